```python
import jax, jax.numpy as jnp
from jax import lax
import numpy as np

D_MODEL = 1024
BATCH = 8
SEQ = 4096
DEPTH = 1

N_HEADS = 16
HEAD_DIM = 64
ATTN_WIDTH = N_HEADS * HEAD_DIM
CONV_WIDTH = D_MODEL
CONV_K = 31
D_FF = 4 * D_MODEL
Q_BLOCK = 128
N_ADA = 6
NORM_EPS = 1e-6

IN_COLS = (ATTN_WIDTH, ATTN_WIDTH, ATTN_WIDTH, N_HEADS, 2 * CONV_WIDTH, 2 * D_MODEL)
IN_SPLITS = tuple(int(s) for s in np.cumsum(IN_COLS)[:-1])
D_IN = int(sum(IN_COLS))

kernel_name = "hybrid_fox_conformer_gated_block"


def rms_norm(x, g):
    xf = x.astype(jnp.float32)
    y = xf * lax.rsqrt(jnp.mean(xf * xf, axis=-1, keepdims=True) + NORM_EPS)
    return (y * g.astype(jnp.float32)).astype(x.dtype)


def layer_norm(x, g, b):
    xf = x.astype(jnp.float32)
    mu = jnp.mean(xf, axis=-1, keepdims=True)
    var = jnp.mean(jnp.square(xf - mu), axis=-1, keepdims=True)
    y = (xf - mu) * lax.rsqrt(var + NORM_EPS)
    return (y * g.astype(jnp.float32) + b.astype(jnp.float32)).astype(x.dtype)


def forgetting_attention(q, k, v, log_f):
    B, S, H, Dh = q.shape
    nb = S // Q_BLOCK
    scale = Dh ** -0.5
    f_cum = jnp.cumsum(log_f, axis=1)
    f_key = jnp.transpose(f_cum, (0, 2, 1))
    q_blocks = jnp.transpose(q.reshape(B, nb, Q_BLOCK, H, Dh), (1, 0, 2, 3, 4))
    f_blocks = jnp.transpose(f_cum.reshape(B, nb, Q_BLOCK, H), (1, 0, 3, 2))
    k_pos = jnp.arange(S)

    def one_block(args):
        q_i, f_i, i = args
        s = jnp.einsum('bqhd,bkhd->bhqk', q_i, k).astype(jnp.float32) * scale
        s = s + f_i[..., :, None] - f_key[:, :, None, :]
        q_pos = i * Q_BLOCK + jnp.arange(Q_BLOCK)
        causal = k_pos[None, :] <= q_pos[:, None]
        s = jnp.where(causal[None, None], s, -jnp.inf)
        p = jax.nn.softmax(s, axis=-1)
        return jnp.einsum('bhqk,bkhd->bqhd', p.astype(v.dtype), v)

    out = lax.map(one_block, (q_blocks, f_blocks, jnp.arange(nb)))
    return jnp.transpose(out, (1, 0, 2, 3, 4)).reshape(B, S, H * Dh)


def causal_depthwise_conv(u, w, b):
    K, C = w.shape
    u_pad = jnp.pad(u, ((0, 0), (K - 1, 0), (0, 0)))
    y = lax.conv_general_dilated(
        u_pad, w[:, None, :].astype(u.dtype), window_strides=(1,), padding='VALID',
        dimension_numbers=('NWC', 'WIO', 'NWC'), feature_group_count=C)
    return y + b.astype(u.dtype)


def setup_inputs(seed: int = 0) -> dict:
    key = jax.random.key(seed)
    ks = jax.random.split(key, 20)
    n = jax.random.normal
    D, L = D_MODEL, DEPTH
    return {
        "x": n(ks[0], (BATCH, SEQ, D), jnp.float32),
        "c": n(ks[1], (BATCH, D), jnp.float32),
        "w_ada": n(ks[2], (L, D, N_ADA * D), jnp.float32) * (0.5 * D ** -0.5),
        "b_ada": n(ks[3], (L, N_ADA * D), jnp.float32) * 0.02,
        "norm1_g": 1.0 + 0.1 * n(ks[4], (L, D), jnp.float32),
        "w_in": n(ks[5], (L, D, D_IN), jnp.float32) * D ** -0.5,
        "b_forget": 3.0 + 0.5 * n(ks[6], (L, N_HEADS), jnp.float32),
        "q_norm_g": 1.0 + 0.1 * n(ks[7], (L, HEAD_DIM), jnp.float32),
        "k_norm_g": 1.0 + 0.1 * n(ks[8], (L, HEAD_DIM), jnp.float32),
        "w_attn_proj": n(ks[9], (L, ATTN_WIDTH, D), jnp.float32) * ATTN_WIDTH ** -0.5,
        "conv_w": n(ks[10], (L, CONV_K, CONV_WIDTH), jnp.float32) * CONV_K ** -0.5,
        "conv_b": 0.02 * n(ks[11], (L, CONV_WIDTH), jnp.float32),
        "conv_ln_g": 1.0 + 0.1 * n(ks[12], (L, CONV_WIDTH), jnp.float32),
        "conv_ln_b": 0.02 * n(ks[13], (L, CONV_WIDTH), jnp.float32),
        "w_conv_proj": n(ks[14], (L, CONV_WIDTH, D), jnp.float32) * CONV_WIDTH ** -0.5,
        "w_out": n(ks[15], (L, D, D), jnp.float32) * D ** -0.5,
        "norm2_g": 1.0 + 0.1 * n(ks[16], (L, D), jnp.float32),
        "w_mlp1": n(ks[17], (L, D, D_FF), jnp.float32) * D ** -0.5,
        "w_mlp2": n(ks[18], (L, D_FF, D), jnp.float32) * D_FF ** -0.5,
    }


def reference(x, c, w_ada, b_ada, norm1_g, w_in, b_forget, q_norm_g, k_norm_g,
              w_attn_proj, conv_w, conv_b, conv_ln_g, conv_ln_b, w_conv_proj,
              w_out, norm2_g, w_mlp1, w_mlp2):
    B, S, D = x.shape
    c_act = jax.nn.silu(c)
    for l in range(DEPTH):
        mod = c_act @ w_ada[l] + b_ada[l]
        sh1, sc1, g1, sh2, sc2, g2 = [m[:, None, :] for m in jnp.split(mod, N_ADA, axis=-1)]

        h = rms_norm(x, norm1_g[l]) * (1.0 + sc1) + sh1
        proj = h @ w_in[l]
        q, k, v, f_logit, glu_in, gate_logit = jnp.split(proj, IN_SPLITS, axis=-1)

        q = rms_norm(q.reshape(B, S, N_HEADS, HEAD_DIM), q_norm_g[l])
        k = rms_norm(k.reshape(B, S, N_HEADS, HEAD_DIM), k_norm_g[l])
        v = v.reshape(B, S, N_HEADS, HEAD_DIM)
        log_f = jax.nn.log_sigmoid(f_logit.astype(jnp.float32) + b_forget[l].astype(jnp.float32))
        branch_a = forgetting_attention(q, k, v, log_f) @ w_attn_proj[l]

        u = glu_in[..., :CONV_WIDTH] * jax.nn.sigmoid(glu_in[..., CONV_WIDTH:])
        u = causal_depthwise_conv(u, conv_w[l], conv_b[l])
        u = jax.nn.silu(layer_norm(u, conv_ln_g[l], conv_ln_b[l]))
        branch_b = u @ w_conv_proj[l]

        gate_a, gate_b = jnp.split(gate_logit, 2, axis=-1)
        merged = jax.nn.sigmoid(gate_a) * branch_a + jax.nn.sigmoid(gate_b) * branch_b
        x = x + g1 * (merged @ w_out[l])

        h2 = rms_norm(x, norm2_g[l]) * (1.0 + sc2) + sh2
        x = x + g2 * (jnp.square(jax.nn.relu(h2 @ w_mlp1[l])) @ w_mlp2[l])
    return x
```

```python
import functools

import jax
import jax.numpy as jnp
from jax import lax
from jax.experimental import pallas as pl
from jax.experimental.pallas import tpu as pltpu

F32 = jnp.float32
BF16 = jnp.bfloat16

N_HEADS = 16
HEAD_DIM = 64
CONV_K = 31
N_ADA = 6
NORM_EPS = 1e-6
LOG2E = 1.4426950408889634
NEG_BIG = -1e30

LANES = 128
MXU_DIM = 256
HALO = 32
VMEM_LIMIT = 56 * 1024 * 1024


def _cparams(n_axes, vmem=VMEM_LIMIT):
    return pltpu.CompilerParams(dimension_semantics=("arbitrary",) * n_axes, vmem_limit_bytes=vmem)


def _const_spec(shape):
    nd = len(shape)
    return pl.BlockSpec(shape, lambda *_: (0,) * nd, pipeline_mode=pl.Buffered(1))


def _sigmoid(x):
    return 1.0 / (1.0 + jnp.exp(-x))


def _dot(a, b):
    return jnp.dot(a, b, preferred_element_type=F32)


def _split3(x):
    hi = x.astype(BF16).astype(F32)
    r = x - hi
    mid = r.astype(BF16).astype(F32)
    lo = r - mid
    return hi, mid, lo


def _ada_kernel(c_ref, w_ref, b_ref, o_ref):
    c = c_ref[...]
    o_ref[...] = _dot(c * _sigmoid(c), w_ref[...]) + b_ref[...]


def _ada(c, w_ada, b_ada):
    B, D = c.shape
    N = w_ada.shape[1]
    tn = 1024
    return pl.pallas_call(
        _ada_kernel,
        grid=(N // tn,),
        in_specs=[pl.BlockSpec((B, D), lambda j: (0, 0)),
                  pl.BlockSpec((D, tn), lambda j: (0, j)),
                  pl.BlockSpec((1, tn), lambda j: (0, j))],
        out_specs=pl.BlockSpec((B, tn), lambda j: (0, j)),
        out_shape=jax.ShapeDtypeStruct((B, N), F32),
        compiler_params=_cparams(1),
        name="ada",
    )(c, w_ada, b_ada.reshape(1, N))


def _inproj_kernel(x_ref, mod_ref, n1g_ref, wq_ref, wk_ref, wv_ref, wf_ref, wga_ref, wgb_ref, wgate_ref,
                   bf_ref, gq_ref, gk_ref, g256_ref, tri_ref, pq_ref, pk_ref,
                   q_out, k_out, v_out, u_out, gate_out, carry_ref):
    tm = x_ref.shape[1]

    @pl.when(pl.program_id(1) == 0)
    def _():
        carry_ref[...] = jnp.zeros_like(carry_ref)

    x = x_ref[0]
    ms = jnp.mean(x * x, axis=-1, keepdims=True)
    sh1 = mod_ref[0, 0:1, :]
    sc1 = mod_ref[0, 1:2, :]
    h = (x * lax.rsqrt(ms + NORM_EPS) * n1g_ref[...]) * (1.0 + sc1) + sh1
    hb = h.astype(BF16)

    lane = lax.broadcasted_iota(jnp.int32, (tm, LANES), 1)

    def pack3(v):
        hi, mid, lo = _split3(v)
        return hi + pltpu.roll(mid, N_HEADS, 1) + pltpu.roll(lo, 2 * N_HEADS, 1)

    z = _dot(hb, wf_ref[...]) + bf_ref[...]
    lf = jnp.minimum(z, 0.0) - jnp.log1p(jnp.exp(-jnp.abs(z)))
    lf = jnp.where(lane < N_HEADS, lf, 0.0)
    cs3 = _dot(tri_ref[...], pack3(lf).astype(BF16))
    cs = cs3 + pltpu.roll(cs3, LANES - N_HEADS, 1) + pltpu.roll(cs3, LANES - 2 * N_HEADS, 1)
    fc = jnp.where(lane < N_HEADS, cs, 0.0) + carry_ref[0:1, :]
    carry_ref[0:1, :] = fc[tm - 1:tm, :]

    feat = jnp.where(lane == 3 * N_HEADS, 1.0, pack3(fc * LOG2E)).astype(BF16)
    augq = _dot(feat, pq_ref[...])
    augk = _dot(feat, pk_ref[...])

    def qk_path(w_ref, g_ref, aug, out):
        a = _dot(hb, w_ref[...])
        for c in range(a.shape[1] // MXU_DIM):
            ac = a[:, c * MXU_DIM:(c + 1) * MXU_DIM]
            ss = _dot((ac * ac).astype(BF16), g256_ref[...])
            an = ac * lax.rsqrt(ss * (1.0 / HEAD_DIM) + NORM_EPS) * g_ref[:, c * MXU_DIM:(c + 1) * MXU_DIM]
            for pp in range(MXU_DIM // LANES):
                blk = an[:, pp * LANES:(pp + 1) * LANES]
                h0 = 2 * (c * (MXU_DIM // LANES) + pp)
                out[0, h0] = jnp.where(lane < HEAD_DIM, blk, aug[:, h0 * LANES:(h0 + 1) * LANES]).astype(BF16)
                out[0, h0 + 1] = jnp.where(lane < HEAD_DIM, pltpu.roll(blk, HEAD_DIM, 1),
                                           aug[:, (h0 + 1) * LANES:(h0 + 2) * LANES]).astype(BF16)

    qk_path(wq_ref, gq_ref, augq, q_out)
    qk_path(wk_ref, gk_ref, augk, k_out)

    va = _dot(hb, wv_ref[...])
    ones_lane = jnp.where(lane == HEAD_DIM, 1.0, 0.0)
    for p in range(N_HEADS // 2):
        blk = va[:, p * LANES:(p + 1) * LANES]
        v_out[0, 2 * p] = jnp.where(lane < HEAD_DIM, blk, ones_lane).astype(BF16)
        v_out[0, 2 * p + 1] = jnp.where(lane < HEAD_DIM, pltpu.roll(blk, HEAD_DIM, 1), ones_lane).astype(BF16)

    u_out[0] = _dot(hb, wga_ref[...]) * _sigmoid(_dot(hb, wgb_ref[...]))
    gate_out[0] = _sigmoid(_dot(hb, wgate_ref[...])).astype(BF16)


def _placement(sign_q):
    rows = jnp.arange(LANES)[:, None]
    cols = jnp.arange(N_HEADS * LANES)[None, :]
    head = cols // LANES
    slot = cols % LANES - HEAD_DIM
    part = jnp.where(sign_q, slot, slot - 3)
    is_val = (part >= 0) & (part < 3) & (rows == part * N_HEADS + head)
    one_slot = jnp.where(sign_q, slot - 3, slot)
    is_one = (one_slot >= 0) & (one_slot < 3) & (rows == 3 * N_HEADS)
    val = jnp.where(sign_q, 1.0, -1.0)
    return (jnp.where(is_val, val, 0.0) + jnp.where(is_one, 1.0, 0.0)).astype(BF16)


def _inproj(x, mod3, n1g, w_in, b_forget, q_norm_g, k_norm_g, tm):
    B, S, D = x.shape
    aw = N_HEADS * HEAD_DIM
    wq = w_in[:, :aw].astype(BF16)
    wk = w_in[:, aw:2 * aw].astype(BF16)
    wv = w_in[:, 2 * aw:3 * aw].astype(BF16)
    o = 3 * aw
    wf = jnp.pad(w_in[:, o:o + N_HEADS], ((0, 0), (0, LANES - N_HEADS))).astype(BF16)
    o += N_HEADS
    wga = w_in[:, o:o + D].astype(BF16)
    wgb = w_in[:, o + D:o + 2 * D].astype(BF16)
    o += 2 * D
    wgate = w_in[:, o:o + 2 * D].astype(BF16)
    bf = jnp.pad(b_forget, (0, LANES - N_HEADS)).reshape(1, LANES)
    gq = (jnp.tile(q_norm_g, N_HEADS) * (HEAD_DIM ** -0.5 * LOG2E)).reshape(1, aw)
    gk = jnp.tile(k_norm_g, N_HEADS).reshape(1, aw)
    r = jnp.arange(MXU_DIM)
    g256 = (r[:, None] // HEAD_DIM == r[None, :] // HEAD_DIM).astype(BF16)
    t = jnp.arange(tm)
    tri = (t[None, :] <= t[:, None]).astype(BF16)
    pq = _placement(True)
    pk = _placement(False)

    head_spec = pl.BlockSpec((1, N_HEADS, tm, LANES), lambda b, i: (b, 0, i, 0))
    head_shape = jax.ShapeDtypeStruct((B, N_HEADS, S, LANES), BF16)
    return pl.pallas_call(
        _inproj_kernel,
        grid=(B, S // tm),
        in_specs=[pl.BlockSpec((1, tm, D), lambda b, i: (b, i, 0)),
                  pl.BlockSpec((1, N_ADA, D), lambda b, i: (b, 0, 0)),
                  _const_spec((1, D)),
                  _const_spec(wq.shape), _const_spec(wk.shape), _const_spec(wv.shape), _const_spec(wf.shape),
                  _const_spec(wga.shape), _const_spec(wgb.shape), _const_spec(wgate.shape),
                  _const_spec(bf.shape), _const_spec(gq.shape), _const_spec(gk.shape),
                  _const_spec(g256.shape), _const_spec(tri.shape), _const_spec(pq.shape), _const_spec(pk.shape)],
        out_specs=[head_spec, head_spec, head_spec,
                   pl.BlockSpec((1, tm, D), lambda b, i: (b, i, 0)),
                   pl.BlockSpec((1, tm, 2 * D), lambda b, i: (b, i, 0))],
        out_shape=[head_shape, head_shape, head_shape,
                   jax.ShapeDtypeStruct((B, S, D), F32),
                   jax.ShapeDtypeStruct((B, S, 2 * D), BF16)],
        scratch_shapes=[pltpu.VMEM((8, LANES), F32)],
        compiler_params=_cparams(2),
        name="inproj",
    )(x, mod3, n1g.reshape(1, D), wq, wk, wv, wf, wga, wgb, wgate, bf, gq, gk, g256, tri, pq, pk)


def _attn_kernel(q_ref, k_ref, v_ref, o_ref):
    tq = q_ref.shape[2]
    i = pl.program_id(2)
    lane = lax.broadcasted_iota(jnp.int32, (tq, LANES), 1)
    row = lax.broadcasted_iota(jnp.int32, (tq, tq), 0)
    col = lax.broadcasted_iota(jnp.int32, (tq, tq), 1)
    outs = []
    for e in range(2):
        q = q_ref[0, e]

        def step(j, carry, masked):
            m, acc = carry
            off = pl.multiple_of(j * tq, tq)
            k = k_ref[0, e, pl.ds(off, tq), :]
            v = v_ref[0, e, pl.ds(off, tq), :]
            s = lax.dot_general(q, k, (((1,), (1,)), ((), ())), preferred_element_type=F32)
            if masked:
                s = jnp.where(col <= row, s, NEG_BIG)
            m_new = jnp.maximum(m, jnp.max(s, axis=-1, keepdims=True))
            p = jnp.exp2(s - m_new)
            acc = jnp.exp2(m - m_new) * acc + _dot(p.astype(BF16), v)
            return m_new, acc

        init = (jnp.full((tq, 1), NEG_BIG, F32), jnp.zeros((tq, LANES), F32))
        carry = lax.fori_loop(0, i, functools.partial(step, masked=False), init)
        _, acc = step(i, carry, masked=True)
        outs.append(acc / acc[:, HEAD_DIM:HEAD_DIM + 1])
    o_ref[0] = jnp.where(lane < HEAD_DIM, outs[0], pltpu.roll(outs[1], HEAD_DIM, 1)).astype(BF16)


def _attention(q, k, v, tq):
    B, H, S, _ = q.shape
    kv_spec = pl.BlockSpec((1, 2, S, LANES), lambda b, p, i: (b, p, 0, 0))
    return pl.pallas_call(
        _attn_kernel,
        grid=(B, H // 2, S // tq),
        in_specs=[pl.BlockSpec((1, 2, tq, LANES), lambda b, p, i: (b, p, i, 0)), kv_spec, kv_spec],
        out_specs=pl.BlockSpec((1, tq, LANES), lambda b, p, i: (b, i, p)),
        out_shape=jax.ShapeDtypeStruct((B, S, H * HEAD_DIM), BF16),
        compiler_params=_cparams(3),
        name="attn",
    )(q, k, v)


def _conv_kernel(u_ref, halo_ref, w_ref, cb_ref, lg_ref, lb_ref, o_ref, ext_ref):
    ts = u_ref.shape[1]
    first = pl.program_id(1) == 0
    ext_ref[0:HALO, :] = jnp.where(first, 0.0, halo_ref[0])
    ext_ref[HALO:, :] = u_ref[0]
    acc = jnp.broadcast_to(cb_ref[...], (ts, u_ref.shape[2]))
    for kk in range(CONV_K):
        acc = acc + w_ref[kk:kk + 1, :] * ext_ref[pl.ds(HALO - (CONV_K - 1) + kk, ts), :]
    mu = jnp.mean(acc, axis=-1, keepdims=True)
    d = acc - mu
    var = jnp.mean(d * d, axis=-1, keepdims=True)
    y = d * lax.rsqrt(var + NORM_EPS) * lg_ref[...] + lb_ref[...]
    o_ref[0] = (y * _sigmoid(y)).astype(BF16)


def _conv_module(u, conv_w, conv_b, ln_g, ln_b, ts):
    B, S, C = u.shape
    w = jnp.pad(conv_w, ((0, HALO - CONV_K), (0, 0)))
    hb = ts // HALO
    return pl.pallas_call(
        _conv_kernel,
        grid=(B, S // ts),
        in_specs=[pl.BlockSpec((1, ts, C), lambda b, i: (b, i, 0)),
                  pl.BlockSpec((1, HALO, C), lambda b, i: (b, jnp.maximum(i * hb - 1, 0), 0)),
                  _const_spec(w.shape), _const_spec((1, C)), _const_spec((1, C)), _const_spec((1, C))],
        out_specs=pl.BlockSpec((1, ts, C), lambda b, i: (b, i, 0)),
        out_shape=jax.ShapeDtypeStruct((B, S, C), BF16),
        scratch_shapes=[pltpu.VMEM((ts + HALO, C), F32)],
        compiler_params=_cparams(2),
        name="conv",
    )(u, u, w, conv_b.reshape(1, C), ln_g.reshape(1, C), ln_b.reshape(1, C))


def _merge_kernel(a_ref, cu_ref, gate_ref, x_ref, mod_ref, n2g_ref, wa_ref, wc_ref, wo_ref, y_out, h2_out):
    D = x_ref.shape[2]
    ba = _dot(a_ref[0], wa_ref[...])
    bb = _dot(cu_ref[0], wc_ref[...])
    merged = gate_ref[0, :, :D].astype(F32) * ba + gate_ref[0, :, D:].astype(F32) * bb
    g1 = mod_ref[0, 2:3, :]
    y = x_ref[0] + g1 * _dot(merged.astype(BF16), wo_ref[...])
    y_out[0] = y
    ms = jnp.mean(y * y, axis=-1, keepdims=True)
    sh2 = mod_ref[0, 3:4, :]
    sc2 = mod_ref[0, 4:5, :]
    h2 = (y * lax.rsqrt(ms + NORM_EPS) * n2g_ref[...]) * (1.0 + sc2) + sh2
    h2_out[0] = h2.astype(BF16)


def _merge(attn, cu, gates, x, mod3, n2g, w_attn_proj, w_conv_proj, w_out, tm):
    B, S, D = x.shape
    tok = lambda w: pl.BlockSpec((1, tm, w), lambda b, i: (b, i, 0))
    return pl.pallas_call(
        _merge_kernel,
        grid=(B, S // tm),
        in_specs=[tok(D), tok(D), tok(2 * D), tok(D),
                  pl.BlockSpec((1, N_ADA, D), lambda b, i: (b, 0, 0)),
                  _const_spec((1, D)), _const_spec((D, D)), _const_spec((D, D)), _const_spec((D, D))],
        out_specs=[tok(D), tok(D)],
        out_shape=[jax.ShapeDtypeStruct((B, S, D), F32), jax.ShapeDtypeStruct((B, S, D), BF16)],
        compiler_params=_cparams(2),
        name="merge",
    )(attn, cu, gates, x, mod3, n2g.reshape(1, D),
      w_attn_proj.astype(BF16), w_conv_proj.astype(BF16), w_out.astype(BF16))


def _mlp_kernel(h2_ref, y_ref, mod_ref, w1_ref, w2_ref, o_ref):
    D = y_ref.shape[2]
    h2 = h2_ref[0]
    acc = jnp.zeros(y_ref.shape[1:], F32)
    for c in range(w1_ref.shape[1] // D):
        hc = jnp.maximum(_dot(h2, w1_ref[:, c * D:(c + 1) * D]), 0.0)
        acc = acc + _dot((hc * hc).astype(BF16), w2_ref[c * D:(c + 1) * D, :])
    g2 = mod_ref[0, 5:6, :]
    o_ref[0] = y_ref[0] + g2 * acc


def _mlp(h2, y, mod3, w1, w2, tm):
    B, S, D = y.shape
    tok = pl.BlockSpec((1, tm, D), lambda b, i: (b, i, 0))
    return pl.pallas_call(
        _mlp_kernel,
        grid=(B, S // tm),
        in_specs=[tok, tok, pl.BlockSpec((1, N_ADA, D), lambda b, i: (b, 0, 0)),
                  _const_spec(w1.shape), _const_spec(w2.shape)],
        out_specs=tok,
        out_shape=jax.ShapeDtypeStruct((B, S, D), F32),
        compiler_params=_cparams(2),
        name="mlp",
    )(h2, y, mod3, w1.astype(BF16), w2.astype(BF16))


def kernel(x, c, w_ada, b_ada, norm1_g, w_in, b_forget, q_norm_g, k_norm_g, w_attn_proj, conv_w, conv_b,
           conv_ln_g, conv_ln_b, w_conv_proj, w_out, norm2_g, w_mlp1, w_mlp2):
    B, S, D = x.shape
    assert D == N_HEADS * HEAD_DIM and w_ada.shape[0] == 1
    tm_in = min(256, S)
    t_attn = min(512, S)
    t_conv = min(256, S)
    t_tok = min(512, S)
    mod3 = _ada(c, w_ada[0], b_ada[0]).reshape(B, N_ADA, D)
    q, k, v, u, gates = _inproj(x, mod3, norm1_g[0], w_in[0], b_forget[0], q_norm_g[0], k_norm_g[0], tm_in)
    attn = _attention(q, k, v, t_attn)
    cu = _conv_module(u, conv_w[0], conv_b[0], conv_ln_g[0], conv_ln_b[0], t_conv)
    y, h2 = _merge(attn, cu, gates, x, mod3, norm2_g[0], w_attn_proj[0], w_conv_proj[0], w_out[0], t_tok)
    return _mlp(h2, y, mod3, w_mlp1[0], w_mlp2[0], t_tok)
```

```python
import functools

import jax
import jax.numpy as jnp
from jax import lax
from jax.experimental import pallas as pl
from jax.experimental.pallas import tpu as pltpu

F32 = jnp.float32
BF16 = jnp.bfloat16

N_HEADS = 16
HEAD_DIM = 64
CONV_K = 31
N_ADA = 6
NORM_EPS = 1e-6
LOG2E = 1.4426950408889634
NEG_BIG = -1e30

LANES = 128
MXU_DIM = 256
HALO = 32
VMEM_LIMIT = 56 * 1024 * 1024


def _cparams(n_axes, vmem=VMEM_LIMIT):
    return pltpu.CompilerParams(dimension_semantics=("arbitrary",) * n_axes, vmem_limit_bytes=vmem)


def _const_spec(shape):
    nd = len(shape)
    return pl.BlockSpec(shape, lambda *_: (0,) * nd, pipeline_mode=pl.Buffered(1))


def _sigmoid(x):
    return 1.0 / (1.0 + jnp.exp(-x))


def _dot(a, b):
    return jnp.dot(a, b, preferred_element_type=F32)


def _split3(x):
    hi = x.astype(BF16).astype(F32)
    r = x - hi
    mid = r.astype(BF16).astype(F32)
    lo = r - mid
    return hi, mid, lo


def _ada_kernel(c_ref, w_ref, b_ref, o_ref):
    c = c_ref[...]
    o_ref[...] = _dot(c * _sigmoid(c), w_ref[...]) + b_ref[...]


def _ada(c, w_ada, b_ada):
    B, D = c.shape
    N = w_ada.shape[1]
    tn = 1024
    return pl.pallas_call(
        _ada_kernel,
        grid=(N // tn,),
        in_specs=[pl.BlockSpec((B, D), lambda j: (0, 0)),
                  pl.BlockSpec((D, tn), lambda j: (0, j)),
                  pl.BlockSpec((1, tn), lambda j: (0, j))],
        out_specs=pl.BlockSpec((B, tn), lambda j: (0, j)),
        out_shape=jax.ShapeDtypeStruct((B, N), F32),
        compiler_params=_cparams(1),
        name="ada",
    )(c, w_ada, b_ada.reshape(1, N))


def _inproj_kernel(x_ref, mod_ref, n1g_ref, wq_ref, wk_ref, wv_ref, wf_ref, wga_ref, wgb_ref, wgate_ref,
                   bf_ref, gq_ref, gk_ref, g256_ref, tri_ref, pq_ref, pk_ref,
                   q_out, k_out, v_out, u_out, gate_out, carry_ref):
    tm = x_ref.shape[1]

    @pl.when(pl.program_id(1) == 0)
    def _():
        carry_ref[...] = jnp.zeros_like(carry_ref)

    x = x_ref[0]
    ms = jnp.mean(x * x, axis=-1, keepdims=True)
    sh1 = mod_ref[0, 0:1, :]
    sc1 = mod_ref[0, 1:2, :]
    h = (x * lax.rsqrt(ms + NORM_EPS) * n1g_ref[...]) * (1.0 + sc1) + sh1
    hb = h.astype(BF16)

    lane = lax.broadcasted_iota(jnp.int32, (tm, LANES), 1)

    def pack3(v):
        hi, mid, lo = _split3(v)
        return hi + pltpu.roll(mid, N_HEADS, 1) + pltpu.roll(lo, 2 * N_HEADS, 1)

    z = _dot(hb, wf_ref[...]) + bf_ref[...]
    lf = jnp.minimum(z, 0.0) - jnp.log1p(jnp.exp(-jnp.abs(z)))
    lf = jnp.where(lane < N_HEADS, lf, 0.0)
    cs3 = _dot(tri_ref[...], pack3(lf).astype(BF16))
    cs = cs3 + pltpu.roll(cs3, LANES - N_HEADS, 1) + pltpu.roll(cs3, LANES - 2 * N_HEADS, 1)
    fc = jnp.where(lane < N_HEADS, cs, 0.0) + carry_ref[0:1, :]
    carry_ref[0:1, :] = fc[tm - 1:tm, :]

    feat = jnp.where(lane == 3 * N_HEADS, 1.0, pack3(fc * LOG2E)).astype(BF16)
    augq = _dot(feat, pq_ref[...])
    augk = _dot(feat, pk_ref[...])

    def qk_path(w_ref, g_ref, aug, out):
        a = _dot(hb, w_ref[...])
        for c in range(a.shape[1] // MXU_DIM):
            ac = a[:, c * MXU_DIM:(c + 1) * MXU_DIM]
            ss = _dot((ac * ac).astype(BF16), g256_ref[...])
            an = ac * lax.rsqrt(ss * (1.0 / HEAD_DIM) + NORM_EPS) * g_ref[:, c * MXU_DIM:(c + 1) * MXU_DIM]
            for pp in range(MXU_DIM // LANES):
                blk = an[:, pp * LANES:(pp + 1) * LANES]
                h0 = 2 * (c * (MXU_DIM // LANES) + pp)
                out[0, h0] = jnp.where(lane < HEAD_DIM, blk, aug[:, h0 * LANES:(h0 + 1) * LANES]).astype(BF16)
                out[0, h0 + 1] = jnp.where(lane < HEAD_DIM, pltpu.roll(blk, HEAD_DIM, 1),
                                           aug[:, (h0 + 1) * LANES:(h0 + 2) * LANES]).astype(BF16)

    qk_path(wq_ref, gq_ref, augq, q_out)
    qk_path(wk_ref, gk_ref, augk, k_out)

    va = _dot(hb, wv_ref[...])
    ones_lane = jnp.where(lane == HEAD_DIM, 1.0, 0.0)
    for p in range(N_HEADS // 2):
        blk = va[:, p * LANES:(p + 1) * LANES]
        v_out[0, 2 * p] = jnp.where(lane < HEAD_DIM, blk, ones_lane).astype(BF16)
        v_out[0, 2 * p + 1] = jnp.where(lane < HEAD_DIM, pltpu.roll(blk, HEAD_DIM, 1), ones_lane).astype(BF16)

    u_out[0] = _dot(hb, wga_ref[...]) * _sigmoid(_dot(hb, wgb_ref[...]))
    gate_out[0] = _sigmoid(_dot(hb, wgate_ref[...])).astype(BF16)


def _placement(sign_q):
    rows = jnp.arange(LANES)[:, None]
    cols = jnp.arange(N_HEADS * LANES)[None, :]
    head = cols // LANES
    slot = cols % LANES - HEAD_DIM
    part = jnp.where(sign_q, slot, slot - 3)
    is_val = (part >= 0) & (part < 3) & (rows == part * N_HEADS + head)
    one_slot = jnp.where(sign_q, slot - 3, slot)
    is_one = (one_slot >= 0) & (one_slot < 3) & (rows == 3 * N_HEADS)
    val = jnp.where(sign_q, 1.0, -1.0)
    return (jnp.where(is_val, val, 0.0) + jnp.where(is_one, 1.0, 0.0)).astype(BF16)


def _inproj(x, mod3, n1g, w_in, b_forget, q_norm_g, k_norm_g, tm):
    B, S, D = x.shape
    aw = N_HEADS * HEAD_DIM
    wq = w_in[:, :aw].astype(BF16)
    wk = w_in[:, aw:2 * aw].astype(BF16)
    wv = w_in[:, 2 * aw:3 * aw].astype(BF16)
    o = 3 * aw
    wf = jnp.pad(w_in[:, o:o + N_HEADS], ((0, 0), (0, LANES - N_HEADS))).astype(BF16)
    o += N_HEADS
    wga = w_in[:, o:o + D].astype(BF16)
    wgb = w_in[:, o + D:o + 2 * D].astype(BF16)
    o += 2 * D
    wgate = w_in[:, o:o + 2 * D].astype(BF16)
    bf = jnp.pad(b_forget, (0, LANES - N_HEADS)).reshape(1, LANES)
    gq = (jnp.tile(q_norm_g, N_HEADS) * (HEAD_DIM ** -0.5 * LOG2E)).reshape(1, aw)
    gk = jnp.tile(k_norm_g, N_HEADS).reshape(1, aw)
    r = jnp.arange(MXU_DIM)
    g256 = (r[:, None] // HEAD_DIM == r[None, :] // HEAD_DIM).astype(BF16)
    t = jnp.arange(tm)
    tri = (t[None, :] <= t[:, None]).astype(BF16)
    pq = _placement(True)
    pk = _placement(False)

    head_spec = pl.BlockSpec((1, N_HEADS, tm, LANES), lambda b, i: (b, 0, i, 0))
    head_shape = jax.ShapeDtypeStruct((B, N_HEADS, S, LANES), BF16)
    return pl.pallas_call(
        _inproj_kernel,
        grid=(B, S // tm),
        in_specs=[pl.BlockSpec((1, tm, D), lambda b, i: (b, i, 0)),
                  pl.BlockSpec((1, N_ADA, D), lambda b, i: (b, 0, 0)),
                  _const_spec((1, D)),
                  _const_spec(wq.shape), _const_spec(wk.shape), _const_spec(wv.shape), _const_spec(wf.shape),
                  _const_spec(wga.shape), _const_spec(wgb.shape), _const_spec(wgate.shape),
                  _const_spec(bf.shape), _const_spec(gq.shape), _const_spec(gk.shape),
                  _const_spec(g256.shape), _const_spec(tri.shape), _const_spec(pq.shape), _const_spec(pk.shape)],
        out_specs=[head_spec, head_spec, head_spec,
                   pl.BlockSpec((1, tm, D), lambda b, i: (b, i, 0)),
                   pl.BlockSpec((1, tm, 2 * D), lambda b, i: (b, i, 0))],
        out_shape=[head_shape, head_shape, head_shape,
                   jax.ShapeDtypeStruct((B, S, D), F32),
                   jax.ShapeDtypeStruct((B, S, 2 * D), BF16)],
        scratch_shapes=[pltpu.VMEM((8, LANES), F32)],
        compiler_params=_cparams(2),
        name="inproj",
    )(x, mod3, n1g.reshape(1, D), wq, wk, wv, wf, wga, wgb, wgate, bf, gq, gk, g256, tri, pq, pk)


def _attn_kernel(q_ref, k_ref, v_ref, o_ref):
    tq = q_ref.shape[2]
    half = tq // 2
    i = pl.program_id(2)
    lane = lax.broadcasted_iota(jnp.int32, (tq, LANES), 1)
    qs = [q_ref[0, e] for e in range(2)]

    def scores(q, k):
        return lax.dot_general(q, k, (((1,), (1,)), ((), ())), preferred_element_type=F32)

    def online_softmax(s, m, acc, v):
        m_new = jnp.maximum(m, jnp.max(s, axis=-1, keepdims=True))
        p = jnp.exp2(s - m_new)
        return m_new, jnp.exp2(m - m_new) * acc + _dot(p.astype(BF16), v)

    def body(j, carry):
        off = pl.multiple_of(j * tq, tq)
        new = []
        for e in range(2):
            m, acc = carry[e]
            s = scores(qs[e], k_ref[0, e, pl.ds(off, tq), :])
            new.append(online_softmax(s, m, acc, v_ref[0, e, pl.ds(off, tq), :]))
        return tuple(new)

    init = (jnp.full((tq, 1), NEG_BIG, F32), jnp.zeros((tq, LANES), F32))
    carry = lax.fori_loop(0, i, body, (init, init))

    off = pl.multiple_of(i * tq, tq)
    row_t = lax.broadcasted_iota(jnp.int32, (half, half), 0)
    col_t = lax.broadcasted_iota(jnp.int32, (half, half), 1)
    row_b = lax.broadcasted_iota(jnp.int32, (half, tq), 0) + half
    col_b = lax.broadcasted_iota(jnp.int32, (half, tq), 1)
    outs = []
    for e in range(2):
        m, acc = carry[e]
        k = k_ref[0, e, pl.ds(off, tq), :]
        v = v_ref[0, e, pl.ds(off, tq), :]
        s_t = jnp.where(col_t <= row_t, scores(qs[e][:half], k[:half]), NEG_BIG)
        _, acc_t = online_softmax(s_t, m[:half], acc[:half], v[:half])
        s_b = jnp.where(col_b <= row_b, scores(qs[e][half:], k), NEG_BIG)
        _, acc_b = online_softmax(s_b, m[half:], acc[half:], v)
        acc = jnp.concatenate([acc_t, acc_b], axis=0)
        outs.append(acc / acc[:, HEAD_DIM:HEAD_DIM + 1])
    o_ref[0] = jnp.where(lane < HEAD_DIM, outs[0], pltpu.roll(outs[1], HEAD_DIM, 1)).astype(BF16)


def _attention(q, k, v, tq):
    B, H, S, _ = q.shape
    kv_spec = pl.BlockSpec((1, 2, S, LANES), lambda b, p, i: (b, p, 0, 0))
    return pl.pallas_call(
        _attn_kernel,
        grid=(B, H // 2, S // tq),
        in_specs=[pl.BlockSpec((1, 2, tq, LANES), lambda b, p, i: (b, p, i, 0)), kv_spec, kv_spec],
        out_specs=pl.BlockSpec((1, tq, LANES), lambda b, p, i: (b, i, p)),
        out_shape=jax.ShapeDtypeStruct((B, S, H * HEAD_DIM), BF16),
        compiler_params=_cparams(3),
        name="attn",
    )(q, k, v)


def _conv_kernel(u_ref, halo_ref, w_ref, cb_ref, lg_ref, lb_ref, o_ref, ext_ref):
    ts = u_ref.shape[1]
    first = pl.program_id(1) == 0
    ext_ref[0:HALO, :] = jnp.where(first, 0.0, halo_ref[0])
    ext_ref[HALO:, :] = u_ref[0]
    acc = jnp.broadcast_to(cb_ref[...], (ts, u_ref.shape[2]))
    for kk in range(CONV_K):
        acc = acc + w_ref[kk:kk + 1, :] * ext_ref[pl.ds(HALO - (CONV_K - 1) + kk, ts), :]
    mu = jnp.mean(acc, axis=-1, keepdims=True)
    d = acc - mu
    var = jnp.mean(d * d, axis=-1, keepdims=True)
    y = d * lax.rsqrt(var + NORM_EPS) * lg_ref[...] + lb_ref[...]
    o_ref[0] = (y * _sigmoid(y)).astype(BF16)


def _conv_module(u, conv_w, conv_b, ln_g, ln_b, ts):
    B, S, C = u.shape
    w = jnp.pad(conv_w, ((0, HALO - CONV_K), (0, 0)))
    hb = ts // HALO
    return pl.pallas_call(
        _conv_kernel,
        grid=(B, S // ts),
        in_specs=[pl.BlockSpec((1, ts, C), lambda b, i: (b, i, 0)),
                  pl.BlockSpec((1, HALO, C), lambda b, i: (b, jnp.maximum(i * hb - 1, 0), 0)),
                  _const_spec(w.shape), _const_spec((1, C)), _const_spec((1, C)), _const_spec((1, C))],
        out_specs=pl.BlockSpec((1, ts, C), lambda b, i: (b, i, 0)),
        out_shape=jax.ShapeDtypeStruct((B, S, C), BF16),
        scratch_shapes=[pltpu.VMEM((ts + HALO, C), F32)],
        compiler_params=_cparams(2),
        name="conv",
    )(u, u, w, conv_b.reshape(1, C), ln_g.reshape(1, C), ln_b.reshape(1, C))


def _merge_kernel(a_ref, cu_ref, gate_ref, x_ref, mod_ref, n2g_ref, wa_ref, wc_ref, wo_ref, y_out, h2_out):
    D = x_ref.shape[2]
    ba = _dot(a_ref[0], wa_ref[...])
    bb = _dot(cu_ref[0], wc_ref[...])
    merged = gate_ref[0, :, :D].astype(F32) * ba + gate_ref[0, :, D:].astype(F32) * bb
    g1 = mod_ref[0, 2:3, :]
    y = x_ref[0] + g1 * _dot(merged.astype(BF16), wo_ref[...])
    y_out[0] = y
    ms = jnp.mean(y * y, axis=-1, keepdims=True)
    sh2 = mod_ref[0, 3:4, :]
    sc2 = mod_ref[0, 4:5, :]
    h2 = (y * lax.rsqrt(ms + NORM_EPS) * n2g_ref[...]) * (1.0 + sc2) + sh2
    h2_out[0] = h2.astype(BF16)


def _merge(attn, cu, gates, x, mod3, n2g, w_attn_proj, w_conv_proj, w_out, tm):
    B, S, D = x.shape
    tok = lambda w: pl.BlockSpec((1, tm, w), lambda b, i: (b, i, 0))
    return pl.pallas_call(
        _merge_kernel,
        grid=(B, S // tm),
        in_specs=[tok(D), tok(D), tok(2 * D), tok(D),
                  pl.BlockSpec((1, N_ADA, D), lambda b, i: (b, 0, 0)),
                  _const_spec((1, D)), _const_spec((D, D)), _const_spec((D, D)), _const_spec((D, D))],
        out_specs=[tok(D), tok(D)],
        out_shape=[jax.ShapeDtypeStruct((B, S, D), F32), jax.ShapeDtypeStruct((B, S, D), BF16)],
        compiler_params=_cparams(2),
        name="merge",
    )(attn, cu, gates, x, mod3, n2g.reshape(1, D),
      w_attn_proj.astype(BF16), w_conv_proj.astype(BF16), w_out.astype(BF16))


def _mlp_kernel(h2_ref, y_ref, mod_ref, w1_ref, w2_ref, o_ref):
    D = y_ref.shape[2]
    h2 = h2_ref[0]
    acc = jnp.zeros(y_ref.shape[1:], F32)
    for c in range(w1_ref.shape[1] // D):
        hc = jnp.maximum(_dot(h2, w1_ref[:, c * D:(c + 1) * D]), 0.0)
        acc = acc + _dot((hc * hc).astype(BF16), w2_ref[c * D:(c + 1) * D, :])
    g2 = mod_ref[0, 5:6, :]
    o_ref[0] = y_ref[0] + g2 * acc


def _mlp(h2, y, mod3, w1, w2, tm):
    B, S, D = y.shape
    tok = pl.BlockSpec((1, tm, D), lambda b, i: (b, i, 0))
    return pl.pallas_call(
        _mlp_kernel,
        grid=(B, S // tm),
        in_specs=[tok, tok, pl.BlockSpec((1, N_ADA, D), lambda b, i: (b, 0, 0)),
                  _const_spec(w1.shape), _const_spec(w2.shape)],
        out_specs=tok,
        out_shape=jax.ShapeDtypeStruct((B, S, D), F32),
        compiler_params=_cparams(2),
        name="mlp",
    )(h2, y, mod3, w1.astype(BF16), w2.astype(BF16))


def kernel(x, c, w_ada, b_ada, norm1_g, w_in, b_forget, q_norm_g, k_norm_g, w_attn_proj, conv_w, conv_b,
           conv_ln_g, conv_ln_b, w_conv_proj, w_out, norm2_g, w_mlp1, w_mlp2):
    B, S, D = x.shape
    assert D == N_HEADS * HEAD_DIM and w_ada.shape[0] == 1
    tm_in = min(256, S)
    t_attn = min(1024, S)
    t_conv = min(256, S)
    t_tok = min(512, S)
    mod3 = _ada(c, w_ada[0], b_ada[0]).reshape(B, N_ADA, D)
    q, k, v, u, gates = _inproj(x, mod3, norm1_g[0], w_in[0], b_forget[0], q_norm_g[0], k_norm_g[0], tm_in)
    attn = _attention(q, k, v, t_attn)
    cu = _conv_module(u, conv_w[0], conv_b[0], conv_ln_g[0], conv_ln_b[0], t_conv)
    y, h2 = _merge(attn, cu, gates, x, mod3, norm2_g[0], w_attn_proj[0], w_conv_proj[0], w_out[0], t_tok)
    return _mlp(h2, y, mod3, w_mlp1[0], w_mlp2[0], t_tok)
```

```python
import functools

import jax
import jax.numpy as jnp
from jax import lax
from jax.experimental import pallas as pl
from jax.experimental.pallas import tpu as pltpu

F32 = jnp.float32
BF16 = jnp.bfloat16

N_HEADS = 16
HEAD_DIM = 64
CONV_K = 31
N_ADA = 6
NORM_EPS = 1e-6
LOG2E = 1.4426950408889634
NEG_BIG = -1e30

LANES = 128
SUBLANES = 8
CONV_ROWS = 64
MXU_DIM = 256
HALO = 32
VMEM_LIMIT = 56 * 1024 * 1024


def _cparams(n_axes, vmem=VMEM_LIMIT):
    return pltpu.CompilerParams(dimension_semantics=("arbitrary",) * n_axes, vmem_limit_bytes=vmem)


def _const_spec(shape):
    nd = len(shape)
    return pl.BlockSpec(shape, lambda *_: (0,) * nd, pipeline_mode=pl.Buffered(1))


def _sigmoid(x):
    return 1.0 / (1.0 + jnp.exp(-x))


def _dot(a, b):
    return jnp.dot(a, b, preferred_element_type=F32)


def _split3(x):
    hi = x.astype(BF16).astype(F32)
    r = x - hi
    mid = r.astype(BF16).astype(F32)
    lo = r - mid
    return hi, mid, lo


def _ada_kernel(c_ref, w_ref, b_ref, o_ref):
    c = c_ref[...]
    o_ref[...] = _dot(c * _sigmoid(c), w_ref[...]) + b_ref[...]


def _ada(c, w_ada, b_ada):
    B, D = c.shape
    N = w_ada.shape[1]
    tn = 1024
    return pl.pallas_call(
        _ada_kernel,
        grid=(N // tn,),
        in_specs=[pl.BlockSpec((B, D), lambda j: (0, 0)),
                  pl.BlockSpec((D, tn), lambda j: (0, j)),
                  pl.BlockSpec((1, tn), lambda j: (0, j))],
        out_specs=pl.BlockSpec((B, tn), lambda j: (0, j)),
        out_shape=jax.ShapeDtypeStruct((B, N), F32),
        compiler_params=_cparams(1),
        name="ada",
    )(c, w_ada, b_ada.reshape(1, N))


def _conv_module(u, halo_ref, sh_ref, y_ref, cw_ref, cb_ref):
    tm, C = u.shape
    n = tm + HALO
    ext = jnp.concatenate([halo_ref[...], u], axis=0)
    halo_ref[...] = u[tm - HALO:, :]
    sh_ref[0] = ext
    for r in range(1, SUBLANES):
        sh_ref[r] = pltpu.roll(ext, n - r, 0)
    for lb in range(C // LANES):
        lanes = slice(lb * LANES, (lb + 1) * LANES)
        wv = [jnp.broadcast_to(cw_ref[kk:kk + 1, lanes], (CONV_ROWS, LANES)) for kk in range(CONV_K)]
        bias = jnp.broadcast_to(cb_ref[:, lanes], (CONV_ROWS, LANES))
        for r0 in range(0, tm, CONV_ROWS):
            acc = bias
            for kk in range(CONV_K):
                a, r = divmod(HALO - (CONV_K - 1) + kk, SUBLANES)
                acc = acc + wv[kk] * sh_ref[r, r0 + SUBLANES * a:r0 + SUBLANES * a + CONV_ROWS, lanes]
            y_ref[r0:r0 + CONV_ROWS, lanes] = acc


def _inproj_kernel(x_ref, mod_ref, n1g_ref, wq_ref, wk_ref, wv_ref, wf_ref, wga_ref, wgb_ref, wgate_ref,
                   bf_ref, gq_ref, gk_ref, g256_ref, tri_ref, pq_ref, pk_ref,
                   cw_ref, cb_ref, lg_ref, lb_ref,
                   q_out, k_out, v_out, cu_out, gate_out, carry_ref, halo_ref, sh_ref, y_ref):
    tm = x_ref.shape[1]

    @pl.when(pl.program_id(1) == 0)
    def _():
        carry_ref[...] = jnp.zeros_like(carry_ref)
        halo_ref[...] = jnp.zeros_like(halo_ref)

    x = x_ref[0]
    ms = jnp.mean(x * x, axis=-1, keepdims=True)
    sh1 = mod_ref[0, 0:1, :]
    sc1 = mod_ref[0, 1:2, :]
    h = (x * lax.rsqrt(ms + NORM_EPS) * n1g_ref[...]) * (1.0 + sc1) + sh1
    hb = h.astype(BF16)

    u = _dot(hb, wga_ref[...]) * _sigmoid(_dot(hb, wgb_ref[...]))
    _conv_module(u, halo_ref, sh_ref, y_ref, cw_ref, cb_ref)
    yc = y_ref[...]
    mu = jnp.mean(yc, axis=-1, keepdims=True)
    dc = yc - mu
    var = jnp.mean(dc * dc, axis=-1, keepdims=True)
    yn = dc * lax.rsqrt(var + NORM_EPS) * lg_ref[...] + lb_ref[...]
    cu_out[0] = (yn * _sigmoid(yn)).astype(BF16)

    lane = lax.broadcasted_iota(jnp.int32, (tm, LANES), 1)

    def pack3(v):
        hi, mid, lo = _split3(v)
        return hi + pltpu.roll(mid, N_HEADS, 1) + pltpu.roll(lo, 2 * N_HEADS, 1)

    z = _dot(hb, wf_ref[...]) + bf_ref[...]
    lf = jnp.minimum(z, 0.0) - jnp.log1p(jnp.exp(-jnp.abs(z)))
    lf = jnp.where(lane < N_HEADS, lf, 0.0)
    cs3 = _dot(tri_ref[...], pack3(lf).astype(BF16))
    cs = cs3 + pltpu.roll(cs3, LANES - N_HEADS, 1) + pltpu.roll(cs3, LANES - 2 * N_HEADS, 1)
    fc = jnp.where(lane < N_HEADS, cs, 0.0) + carry_ref[0:1, :]
    carry_ref[0:1, :] = fc[tm - 1:tm, :]

    feat = jnp.where(lane == 3 * N_HEADS, 1.0, pack3(fc * LOG2E)).astype(BF16)
    augq = _dot(feat, pq_ref[...])
    augk = _dot(feat, pk_ref[...])

    def qk_path(w_ref, g_ref, aug, out):
        a = _dot(hb, w_ref[...])
        for c in range(a.shape[1] // MXU_DIM):
            ac = a[:, c * MXU_DIM:(c + 1) * MXU_DIM]
            ss = _dot((ac * ac).astype(BF16), g256_ref[...])
            an = ac * lax.rsqrt(ss * (1.0 / HEAD_DIM) + NORM_EPS) * g_ref[:, c * MXU_DIM:(c + 1) * MXU_DIM]
            for pp in range(MXU_DIM // LANES):
                blk = an[:, pp * LANES:(pp + 1) * LANES]
                h0 = 2 * (c * (MXU_DIM // LANES) + pp)
                out[0, h0] = jnp.where(lane < HEAD_DIM, blk, aug[:, h0 * LANES:(h0 + 1) * LANES]).astype(BF16)
                out[0, h0 + 1] = jnp.where(lane < HEAD_DIM, pltpu.roll(blk, HEAD_DIM, 1),
                                           aug[:, (h0 + 1) * LANES:(h0 + 2) * LANES]).astype(BF16)

    qk_path(wq_ref, gq_ref, augq, q_out)
    qk_path(wk_ref, gk_ref, augk, k_out)

    va = _dot(hb, wv_ref[...])
    ones_lane = jnp.where(lane == HEAD_DIM, 1.0, 0.0)
    for p in range(N_HEADS // 2):
        blk = va[:, p * LANES:(p + 1) * LANES]
        v_out[0, 2 * p] = jnp.where(lane < HEAD_DIM, blk, ones_lane).astype(BF16)
        v_out[0, 2 * p + 1] = jnp.where(lane < HEAD_DIM, pltpu.roll(blk, HEAD_DIM, 1), ones_lane).astype(BF16)

    gate_out[0] = _sigmoid(_dot(hb, wgate_ref[...])).astype(BF16)


def _placement(sign_q):
    rows = jnp.arange(LANES)[:, None]
    cols = jnp.arange(N_HEADS * LANES)[None, :]
    head = cols // LANES
    slot = cols % LANES - HEAD_DIM
    part = jnp.where(sign_q, slot, slot - 3)
    is_val = (part >= 0) & (part < 3) & (rows == part * N_HEADS + head)
    one_slot = jnp.where(sign_q, slot - 3, slot)
    is_one = (one_slot >= 0) & (one_slot < 3) & (rows == 3 * N_HEADS)
    val = jnp.where(sign_q, 1.0, -1.0)
    return (jnp.where(is_val, val, 0.0) + jnp.where(is_one, 1.0, 0.0)).astype(BF16)


def _inproj(x, mod3, n1g, w_in, b_forget, q_norm_g, k_norm_g, conv_w, conv_b, ln_g, ln_b, tm):
    B, S, D = x.shape
    aw = N_HEADS * HEAD_DIM
    wq = w_in[:, :aw].astype(BF16)
    wk = w_in[:, aw:2 * aw].astype(BF16)
    wv = w_in[:, 2 * aw:3 * aw].astype(BF16)
    o = 3 * aw
    wf = jnp.pad(w_in[:, o:o + N_HEADS], ((0, 0), (0, LANES - N_HEADS))).astype(BF16)
    o += N_HEADS
    wga = w_in[:, o:o + D].astype(BF16)
    wgb = w_in[:, o + D:o + 2 * D].astype(BF16)
    o += 2 * D
    wgate = w_in[:, o:o + 2 * D].astype(BF16)
    bf = jnp.pad(b_forget, (0, LANES - N_HEADS)).reshape(1, LANES)
    gq = (jnp.tile(q_norm_g, N_HEADS) * (HEAD_DIM ** -0.5 * LOG2E)).reshape(1, aw)
    gk = jnp.tile(k_norm_g, N_HEADS).reshape(1, aw)
    r = jnp.arange(MXU_DIM)
    g256 = (r[:, None] // HEAD_DIM == r[None, :] // HEAD_DIM).astype(BF16)
    t = jnp.arange(tm)
    tri = (t[None, :] <= t[:, None]).astype(BF16)
    pq = _placement(True)
    pk = _placement(False)
    cw = jnp.pad(conv_w, ((0, HALO - CONV_K), (0, 0)))

    head_spec = pl.BlockSpec((1, N_HEADS, tm, LANES), lambda b, i: (b, 0, i, 0))
    head_shape = jax.ShapeDtypeStruct((B, N_HEADS, S, LANES), BF16)
    return pl.pallas_call(
        _inproj_kernel,
        grid=(B, S // tm),
        in_specs=[pl.BlockSpec((1, tm, D), lambda b, i: (b, i, 0)),
                  pl.BlockSpec((1, N_ADA, D), lambda b, i: (b, 0, 0)),
                  _const_spec((1, D)),
                  _const_spec(wq.shape), _const_spec(wk.shape), _const_spec(wv.shape), _const_spec(wf.shape),
                  _const_spec(wga.shape), _const_spec(wgb.shape), _const_spec(wgate.shape),
                  _const_spec(bf.shape), _const_spec(gq.shape), _const_spec(gk.shape),
                  _const_spec(g256.shape), _const_spec(tri.shape), _const_spec(pq.shape), _const_spec(pk.shape),
                  _const_spec(cw.shape), _const_spec((1, D)), _const_spec((1, D)), _const_spec((1, D))],
        out_specs=[head_spec, head_spec, head_spec,
                   pl.BlockSpec((1, tm, D), lambda b, i: (b, i, 0)),
                   pl.BlockSpec((1, tm, 2 * D), lambda b, i: (b, i, 0))],
        out_shape=[head_shape, head_shape, head_shape,
                   jax.ShapeDtypeStruct((B, S, D), BF16),
                   jax.ShapeDtypeStruct((B, S, 2 * D), BF16)],
        scratch_shapes=[pltpu.VMEM((SUBLANES, LANES), F32), pltpu.VMEM((HALO, D), F32),
                        pltpu.VMEM((SUBLANES, tm + HALO, D), F32), pltpu.VMEM((tm, D), F32)],
        compiler_params=_cparams(2),
        name="inproj",
    )(x, mod3, n1g.reshape(1, D), wq, wk, wv, wf, wga, wgb, wgate, bf, gq, gk, g256, tri, pq, pk,
      cw, conv_b.reshape(1, D), ln_g.reshape(1, D), ln_b.reshape(1, D))


def _attn_kernel(q_ref, k_ref, v_ref, o_ref):
    tq = q_ref.shape[2]
    half = tq // 2
    i = pl.program_id(2)
    lane = lax.broadcasted_iota(jnp.int32, (tq, LANES), 1)
    qs = [q_ref[0, e] for e in range(2)]

    def scores(q, k):
        return lax.dot_general(q, k, (((1,), (1,)), ((), ())), preferred_element_type=F32)

    def online_softmax(s, m, acc, v):
        m_new = jnp.maximum(m, jnp.max(s, axis=-1, keepdims=True))
        p = jnp.exp2(s - m_new)
        return m_new, jnp.exp2(m - m_new) * acc + _dot(p.astype(BF16), v)

    def body(j, carry):
        off = pl.multiple_of(j * tq, tq)
        new = []
        for e in range(2):
            m, acc = carry[e]
            s = scores(qs[e], k_ref[0, e, pl.ds(off, tq), :])
            new.append(online_softmax(s, m, acc, v_ref[0, e, pl.ds(off, tq), :]))
        return tuple(new)

    init = (jnp.full((tq, 1), NEG_BIG, F32), jnp.zeros((tq, LANES), F32))
    carry = lax.fori_loop(0, i, body, (init, init))

    off = pl.multiple_of(i * tq, tq)
    row_t = lax.broadcasted_iota(jnp.int32, (half, half), 0)
    col_t = lax.broadcasted_iota(jnp.int32, (half, half), 1)
    row_b = lax.broadcasted_iota(jnp.int32, (half, tq), 0) + half
    col_b = lax.broadcasted_iota(jnp.int32, (half, tq), 1)
    outs = []
    for e in range(2):
        m, acc = carry[e]
        k = k_ref[0, e, pl.ds(off, tq), :]
        v = v_ref[0, e, pl.ds(off, tq), :]
        s_t = jnp.where(col_t <= row_t, scores(qs[e][:half], k[:half]), NEG_BIG)
        _, acc_t = online_softmax(s_t, m[:half], acc[:half], v[:half])
        s_b = jnp.where(col_b <= row_b, scores(qs[e][half:], k), NEG_BIG)
        _, acc_b = online_softmax(s_b, m[half:], acc[half:], v)
        acc = jnp.concatenate([acc_t, acc_b], axis=0)
        outs.append(acc / acc[:, HEAD_DIM:HEAD_DIM + 1])
    o_ref[0] = jnp.where(lane < HEAD_DIM, outs[0], pltpu.roll(outs[1], HEAD_DIM, 1)).astype(BF16)


def _attention(q, k, v, tq):
    B, H, S, _ = q.shape
    kv_spec = pl.BlockSpec((1, 2, S, LANES), lambda b, p, i: (b, p, 0, 0))
    return pl.pallas_call(
        _attn_kernel,
        grid=(B, H // 2, S // tq),
        in_specs=[pl.BlockSpec((1, 2, tq, LANES), lambda b, p, i: (b, p, i, 0)), kv_spec, kv_spec],
        out_specs=pl.BlockSpec((1, tq, LANES), lambda b, p, i: (b, i, p)),
        out_shape=jax.ShapeDtypeStruct((B, S, H * HEAD_DIM), BF16),
        compiler_params=_cparams(3),
        name="attn",
    )(q, k, v)


def _merge_kernel(a_ref, cu_ref, gate_ref, x_ref, mod_ref, n2g_ref, wa_ref, wc_ref, wo_ref, y_out, h2_out):
    D = x_ref.shape[2]
    ba = _dot(a_ref[0], wa_ref[...])
    bb = _dot(cu_ref[0], wc_ref[...])
    merged = gate_ref[0, :, :D].astype(F32) * ba + gate_ref[0, :, D:].astype(F32) * bb
    g1 = mod_ref[0, 2:3, :]
    y = x_ref[0] + g1 * _dot(merged.astype(BF16), wo_ref[...])
    y_out[0] = y
    ms = jnp.mean(y * y, axis=-1, keepdims=True)
    sh2 = mod_ref[0, 3:4, :]
    sc2 = mod_ref[0, 4:5, :]
    h2 = (y * lax.rsqrt(ms + NORM_EPS) * n2g_ref[...]) * (1.0 + sc2) + sh2
    h2_out[0] = h2.astype(BF16)


def _merge(attn, cu, gates, x, mod3, n2g, w_attn_proj, w_conv_proj, w_out, tm):
    B, S, D = x.shape
    tok = lambda w: pl.BlockSpec((1, tm, w), lambda b, i: (b, i, 0))
    return pl.pallas_call(
        _merge_kernel,
        grid=(B, S // tm),
        in_specs=[tok(D), tok(D), tok(2 * D), tok(D),
                  pl.BlockSpec((1, N_ADA, D), lambda b, i: (b, 0, 0)),
                  _const_spec((1, D)), _const_spec((D, D)), _const_spec((D, D)), _const_spec((D, D))],
        out_specs=[tok(D), tok(D)],
        out_shape=[jax.ShapeDtypeStruct((B, S, D), F32), jax.ShapeDtypeStruct((B, S, D), BF16)],
        compiler_params=_cparams(2),
        name="merge",
    )(attn, cu, gates, x, mod3, n2g.reshape(1, D),
      w_attn_proj.astype(BF16), w_conv_proj.astype(BF16), w_out.astype(BF16))


def _mlp_kernel(h2_ref, y_ref, mod_ref, w1_ref, w2_ref, o_ref):
    D = y_ref.shape[2]
    h2 = h2_ref[0]
    acc = jnp.zeros(y_ref.shape[1:], F32)
    for c in range(w1_ref.shape[1] // D):
        hc = jnp.maximum(_dot(h2, w1_ref[:, c * D:(c + 1) * D]), 0.0)
        acc = acc + _dot((hc * hc).astype(BF16), w2_ref[c * D:(c + 1) * D, :])
    g2 = mod_ref[0, 5:6, :]
    o_ref[0] = y_ref[0] + g2 * acc


def _mlp(h2, y, mod3, w1, w2, tm):
    B, S, D = y.shape
    tok = pl.BlockSpec((1, tm, D), lambda b, i: (b, i, 0))
    return pl.pallas_call(
        _mlp_kernel,
        grid=(B, S // tm),
        in_specs=[tok, tok, pl.BlockSpec((1, N_ADA, D), lambda b, i: (b, 0, 0)),
                  _const_spec(w1.shape), _const_spec(w2.shape)],
        out_specs=tok,
        out_shape=jax.ShapeDtypeStruct((B, S, D), F32),
        compiler_params=_cparams(2),
        name="mlp",
    )(h2, y, mod3, w1.astype(BF16), w2.astype(BF16))


def kernel(x, c, w_ada, b_ada, norm1_g, w_in, b_forget, q_norm_g, k_norm_g, w_attn_proj, conv_w, conv_b,
           conv_ln_g, conv_ln_b, w_conv_proj, w_out, norm2_g, w_mlp1, w_mlp2):
    B, S, D = x.shape
    assert D == N_HEADS * HEAD_DIM and w_ada.shape[0] == 1
    tm_in = min(256, S)
    t_attn = min(1024, S)
    t_tok = min(512, S)
    mod3 = _ada(c, w_ada[0], b_ada[0]).reshape(B, N_ADA, D)
    q, k, v, cu, gates = _inproj(x, mod3, norm1_g[0], w_in[0], b_forget[0], q_norm_g[0], k_norm_g[0],
                                 conv_w[0], conv_b[0], conv_ln_g[0], conv_ln_b[0], tm_in)
    attn = _attention(q, k, v, t_attn)
    y, h2 = _merge(attn, cu, gates, x, mod3, norm2_g[0], w_attn_proj[0], w_conv_proj[0], w_out[0], t_tok)
    return _mlp(h2, y, mod3, w_mlp1[0], w_mlp2[0], t_tok)
```

```python
import functools

import jax
import jax.numpy as jnp
from jax import lax
from jax.experimental import pallas as pl
from jax.experimental.pallas import tpu as pltpu

F32 = jnp.float32
BF16 = jnp.bfloat16

N_HEADS = 16
HEAD_DIM = 64
CONV_K = 31
N_ADA = 6
NORM_EPS = 1e-6
LOG2E = 1.4426950408889634
NEG_BIG = -1e30

LANES = 128
SUBLANES = 8
CONV_ROWS = 64
MXU_DIM = 256
HALO = 32
VMEM_LIMIT = 56 * 1024 * 1024


def _cparams(n_axes, vmem=VMEM_LIMIT):
    return pltpu.CompilerParams(dimension_semantics=("arbitrary",) * n_axes, vmem_limit_bytes=vmem)


def _const_spec(shape):
    nd = len(shape)
    return pl.BlockSpec(shape, lambda *_: (0,) * nd, pipeline_mode=pl.Buffered(1))


def _sigmoid(x):
    return 1.0 / (1.0 + jnp.exp(-x))


def _dot(a, b):
    return jnp.dot(a, b, preferred_element_type=F32)


def _split3(x):
    hi = x.astype(BF16).astype(F32)
    r = x - hi
    mid = r.astype(BF16).astype(F32)
    lo = r - mid
    return hi, mid, lo


def _ada_kernel(c_ref, w_ref, b_ref, o_ref):
    c = c_ref[...]
    o_ref[...] = _dot(c * _sigmoid(c), w_ref[...]) + b_ref[...]


def _ada(c, w_ada, b_ada):
    B, D = c.shape
    N = w_ada.shape[1]
    tn = 1024
    return pl.pallas_call(
        _ada_kernel,
        grid=(N // tn,),
        in_specs=[pl.BlockSpec((B, D), lambda j: (0, 0)),
                  pl.BlockSpec((D, tn), lambda j: (0, j)),
                  pl.BlockSpec((1, tn), lambda j: (0, j))],
        out_specs=pl.BlockSpec((B, tn), lambda j: (0, j)),
        out_shape=jax.ShapeDtypeStruct((B, N), F32),
        compiler_params=_cparams(1),
        name="ada",
    )(c, w_ada, b_ada.reshape(1, N))


def _conv_module(u, halo_ref, sh_ref, y_ref, cw_ref, cb_ref):
    tm, C = u.shape
    n = tm + HALO
    ext = jnp.concatenate([halo_ref[...], u], axis=0)
    halo_ref[...] = u[tm - HALO:, :]
    sh_ref[0] = ext
    for r in range(1, SUBLANES):
        sh_ref[r] = pltpu.roll(ext, n - r, 0)
    for lb in range(C // LANES):
        lanes = slice(lb * LANES, (lb + 1) * LANES)
        wv = [jnp.broadcast_to(cw_ref[kk:kk + 1, lanes], (CONV_ROWS, LANES)) for kk in range(CONV_K)]
        bias = jnp.broadcast_to(cb_ref[:, lanes], (CONV_ROWS, LANES))
        for r0 in range(0, tm, CONV_ROWS):
            acc = bias
            for kk in range(CONV_K):
                a, r = divmod(HALO - (CONV_K - 1) + kk, SUBLANES)
                acc = acc + wv[kk] * sh_ref[r, r0 + SUBLANES * a:r0 + SUBLANES * a + CONV_ROWS, lanes]
            y_ref[r0:r0 + CONV_ROWS, lanes] = acc


def _inproj_kernel(x_ref, mod_ref, n1g_ref, wq_ref, wk_ref, wv_ref, wf_ref, wga_ref, wgb_ref, wgate_ref,
                   bf_ref, gq_ref, gk_ref, g256_ref, tri_ref, pq_ref, pk_ref,
                   cw_ref, cb_ref, lg_ref, lb_ref,
                   q_out, k_out, v_out, cu_out, gate_out, carry_ref, halo_ref, sh_ref, y_ref):
    tm = x_ref.shape[1]

    @pl.when(pl.program_id(1) == 0)
    def _():
        carry_ref[...] = jnp.zeros_like(carry_ref)
        halo_ref[...] = jnp.zeros_like(halo_ref)

    x = x_ref[0]
    ms = jnp.mean(x * x, axis=-1, keepdims=True)
    sh1 = mod_ref[0, 0:1, :]
    sc1 = mod_ref[0, 1:2, :]
    h = (x * lax.rsqrt(ms + NORM_EPS) * n1g_ref[...]) * (1.0 + sc1) + sh1
    hb = h.astype(BF16)

    u = _dot(hb, wga_ref[...]) * _sigmoid(_dot(hb, wgb_ref[...]))
    _conv_module(u, halo_ref, sh_ref, y_ref, cw_ref, cb_ref)
    yc = y_ref[...]
    mu = jnp.mean(yc, axis=-1, keepdims=True)
    dc = yc - mu
    var = jnp.mean(dc * dc, axis=-1, keepdims=True)
    yn = dc * lax.rsqrt(var + NORM_EPS) * lg_ref[...] + lb_ref[...]
    cu_out[0] = (yn * _sigmoid(yn)).astype(BF16)

    lane = lax.broadcasted_iota(jnp.int32, (tm, LANES), 1)

    def pack3(v):
        hi, mid, lo = _split3(v)
        return hi + pltpu.roll(mid, N_HEADS, 1) + pltpu.roll(lo, 2 * N_HEADS, 1)

    z = _dot(hb, wf_ref[...]) + bf_ref[...]
    lf = jnp.minimum(z, 0.0) - jnp.log1p(jnp.exp(-jnp.abs(z)))
    lf = jnp.where(lane < N_HEADS, lf, 0.0)
    cs3 = _dot(tri_ref[...], pack3(lf).astype(BF16))
    cs = cs3 + pltpu.roll(cs3, LANES - N_HEADS, 1) + pltpu.roll(cs3, LANES - 2 * N_HEADS, 1)
    fc = jnp.where(lane < N_HEADS, cs, 0.0) + carry_ref[0:1, :]
    carry_ref[0:1, :] = fc[tm - 1:tm, :]

    feat = jnp.where(lane == 3 * N_HEADS, 1.0, pack3(fc * LOG2E)).astype(BF16)
    augq = _dot(feat, pq_ref[...])
    augk = _dot(feat, pk_ref[...])

    def qk_path(w_ref, g_ref, aug, out):
        a = _dot(hb, w_ref[...])
        for c in range(a.shape[1] // MXU_DIM):
            ac = a[:, c * MXU_DIM:(c + 1) * MXU_DIM]
            ss = _dot((ac * ac).astype(BF16), g256_ref[...])
            an = ac * lax.rsqrt(ss * (1.0 / HEAD_DIM) + NORM_EPS) * g_ref[:, c * MXU_DIM:(c + 1) * MXU_DIM]
            for pp in range(MXU_DIM // LANES):
                blk = an[:, pp * LANES:(pp + 1) * LANES]
                h0 = 2 * (c * (MXU_DIM // LANES) + pp)
                out[0, h0] = jnp.where(lane < HEAD_DIM, blk, aug[:, h0 * LANES:(h0 + 1) * LANES]).astype(BF16)
                out[0, h0 + 1] = jnp.where(lane < HEAD_DIM, pltpu.roll(blk, HEAD_DIM, 1),
                                           aug[:, (h0 + 1) * LANES:(h0 + 2) * LANES]).astype(BF16)

    qk_path(wq_ref, gq_ref, augq, q_out)
    qk_path(wk_ref, gk_ref, augk, k_out)

    va = _dot(hb, wv_ref[...])
    ones_lane = jnp.where(lane == HEAD_DIM, 1.0, 0.0)
    for p in range(N_HEADS // 2):
        blk = va[:, p * LANES:(p + 1) * LANES]
        v_out[0, 2 * p] = jnp.where(lane < HEAD_DIM, blk, ones_lane).astype(BF16)
        v_out[0, 2 * p + 1] = jnp.where(lane < HEAD_DIM, pltpu.roll(blk, HEAD_DIM, 1), ones_lane).astype(BF16)

    gate_out[0] = _sigmoid(_dot(hb, wgate_ref[...])).astype(BF16)


def _placement(sign_q):
    rows = jnp.arange(LANES)[:, None]
    cols = jnp.arange(N_HEADS * LANES)[None, :]
    head = cols // LANES
    slot = cols % LANES - HEAD_DIM
    part = jnp.where(sign_q, slot, slot - 3)
    is_val = (part >= 0) & (part < 3) & (rows == part * N_HEADS + head)
    one_slot = jnp.where(sign_q, slot - 3, slot)
    is_one = (one_slot >= 0) & (one_slot < 3) & (rows == 3 * N_HEADS)
    val = jnp.where(sign_q, 1.0, -1.0)
    return (jnp.where(is_val, val, 0.0) + jnp.where(is_one, 1.0, 0.0)).astype(BF16)


def _inproj(x, mod3, n1g, w_in, b_forget, q_norm_g, k_norm_g, conv_w, conv_b, ln_g, ln_b, tm):
    B, S, D = x.shape
    aw = N_HEADS * HEAD_DIM
    wq = w_in[:, :aw].astype(BF16)
    wk = w_in[:, aw:2 * aw].astype(BF16)
    wv = w_in[:, 2 * aw:3 * aw].astype(BF16)
    o = 3 * aw
    wf = jnp.pad(w_in[:, o:o + N_HEADS], ((0, 0), (0, LANES - N_HEADS))).astype(BF16)
    o += N_HEADS
    wga = w_in[:, o:o + D].astype(BF16)
    wgb = w_in[:, o + D:o + 2 * D].astype(BF16)
    o += 2 * D
    wgate = w_in[:, o:o + 2 * D].astype(BF16)
    bf = jnp.pad(b_forget, (0, LANES - N_HEADS)).reshape(1, LANES)
    gq = (jnp.tile(q_norm_g, N_HEADS) * (HEAD_DIM ** -0.5 * LOG2E)).reshape(1, aw)
    gk = jnp.tile(k_norm_g, N_HEADS).reshape(1, aw)
    r = jnp.arange(MXU_DIM)
    g256 = (r[:, None] // HEAD_DIM == r[None, :] // HEAD_DIM).astype(BF16)
    t = jnp.arange(tm)
    tri = (t[None, :] <= t[:, None]).astype(BF16)
    pq = _placement(True)
    pk = _placement(False)
    cw = jnp.pad(conv_w, ((0, HALO - CONV_K), (0, 0)))

    head_spec = pl.BlockSpec((1, N_HEADS, tm, LANES), lambda b, i: (b, 0, i, 0))
    head_shape = jax.ShapeDtypeStruct((B, N_HEADS, S, LANES), BF16)
    return pl.pallas_call(
        _inproj_kernel,
        grid=(B, S // tm),
        in_specs=[pl.BlockSpec((1, tm, D), lambda b, i: (b, i, 0)),
                  pl.BlockSpec((1, N_ADA, D), lambda b, i: (b, 0, 0)),
                  _const_spec((1, D)),
                  _const_spec(wq.shape), _const_spec(wk.shape), _const_spec(wv.shape), _const_spec(wf.shape),
                  _const_spec(wga.shape), _const_spec(wgb.shape), _const_spec(wgate.shape),
                  _const_spec(bf.shape), _const_spec(gq.shape), _const_spec(gk.shape),
                  _const_spec(g256.shape), _const_spec(tri.shape), _const_spec(pq.shape), _const_spec(pk.shape),
                  _const_spec(cw.shape), _const_spec((1, D)), _const_spec((1, D)), _const_spec((1, D))],
        out_specs=[head_spec, head_spec, head_spec,
                   pl.BlockSpec((1, tm, D), lambda b, i: (b, i, 0)),
                   pl.BlockSpec((1, tm, 2 * D), lambda b, i: (b, i, 0))],
        out_shape=[head_shape, head_shape, head_shape,
                   jax.ShapeDtypeStruct((B, S, D), BF16),
                   jax.ShapeDtypeStruct((B, S, 2 * D), BF16)],
        scratch_shapes=[pltpu.VMEM((SUBLANES, LANES), F32), pltpu.VMEM((HALO, D), F32),
                        pltpu.VMEM((SUBLANES, tm + HALO, D), F32), pltpu.VMEM((tm, D), F32)],
        compiler_params=_cparams(2),
        name="inproj",
    )(x, mod3, n1g.reshape(1, D), wq, wk, wv, wf, wga, wgb, wgate, bf, gq, gk, g256, tri, pq, pk,
      cw, conv_b.reshape(1, D), ln_g.reshape(1, D), ln_b.reshape(1, D))


def _attn_kernel(q_ref, k_ref, v_ref, o_ref, *, tq):
    S = q_ref.shape[2]
    half = tq // 2
    lane = lax.broadcasted_iota(jnp.int32, (tq, LANES), 1)
    row_t = lax.broadcasted_iota(jnp.int32, (half, half), 0)
    col_t = lax.broadcasted_iota(jnp.int32, (half, half), 1)
    row_b = lax.broadcasted_iota(jnp.int32, (half, tq), 0) + half
    col_b = lax.broadcasted_iota(jnp.int32, (half, tq), 1)

    def scores(q, k):
        return lax.dot_general(q, k, (((1,), (1,)), ((), ())), preferred_element_type=F32)

    def online_softmax(s, m, acc, v):
        m_new = jnp.maximum(m, jnp.max(s, axis=-1, keepdims=True))
        p = jnp.exp2(s - m_new)
        return m_new, jnp.exp2(m - m_new) * acc + _dot(p.astype(BF16), v)

    for i in range(S // tq):
        rows = slice(i * tq, (i + 1) * tq)
        qs = [q_ref[0, e, rows, :] for e in range(2)]
        carry = [(jnp.full((tq, 1), NEG_BIG, F32), jnp.zeros((tq, LANES), F32))] * 2
        for j in range(i):
            keys = slice(j * tq, (j + 1) * tq)
            carry = [online_softmax(scores(qs[e], k_ref[0, e, keys, :]), *carry[e], v_ref[0, e, keys, :])
                     for e in range(2)]
        outs = []
        for e in range(2):
            m, acc = carry[e]
            k = k_ref[0, e, rows, :]
            v = v_ref[0, e, rows, :]
            s_t = jnp.where(col_t <= row_t, scores(qs[e][:half], k[:half]), NEG_BIG)
            _, acc_t = online_softmax(s_t, m[:half], acc[:half], v[:half])
            s_b = jnp.where(col_b <= row_b, scores(qs[e][half:], k), NEG_BIG)
            _, acc_b = online_softmax(s_b, m[half:], acc[half:], v)
            acc = jnp.concatenate([acc_t, acc_b], axis=0)
            outs.append(acc / acc[:, HEAD_DIM:HEAD_DIM + 1])
        o_ref[0, rows, :] = jnp.where(lane < HEAD_DIM, outs[0], pltpu.roll(outs[1], HEAD_DIM, 1)).astype(BF16)


def _attention(q, k, v, tq):
    B, H, S, _ = q.shape
    pair_spec = pl.BlockSpec((1, 2, S, LANES), lambda b, p: (b, p, 0, 0))
    return pl.pallas_call(
        functools.partial(_attn_kernel, tq=tq),
        grid=(B, H // 2),
        in_specs=[pair_spec, pair_spec, pair_spec],
        out_specs=pl.BlockSpec((1, S, LANES), lambda b, p: (b, 0, p)),
        out_shape=jax.ShapeDtypeStruct((B, S, H * HEAD_DIM), BF16),
        compiler_params=_cparams(2),
        name="attn",
    )(q, k, v)


def _post_kernel(a_ref, cu_ref, gate_ref, x_ref, mod_ref, n2g_ref, wa_ref, wc_ref, wo_ref, w1_ref, w2_ref, o_ref):
    D = x_ref.shape[2]
    ba = _dot(a_ref[0], wa_ref[...])
    bb = _dot(cu_ref[0], wc_ref[...])
    merged = gate_ref[0, :, :D].astype(F32) * ba + gate_ref[0, :, D:].astype(F32) * bb
    g1 = mod_ref[0, 2:3, :]
    y = x_ref[0] + g1 * _dot(merged.astype(BF16), wo_ref[...])
    ms = jnp.mean(y * y, axis=-1, keepdims=True)
    sh2 = mod_ref[0, 3:4, :]
    sc2 = mod_ref[0, 4:5, :]
    h2 = ((y * lax.rsqrt(ms + NORM_EPS) * n2g_ref[...]) * (1.0 + sc2) + sh2).astype(BF16)
    acc = jnp.zeros(y.shape, F32)
    for c in range(w1_ref.shape[1] // D):
        hc = jnp.maximum(_dot(h2, w1_ref[:, c * D:(c + 1) * D]), 0.0)
        acc = acc + _dot((hc * hc).astype(BF16), w2_ref[c * D:(c + 1) * D, :])
    g2 = mod_ref[0, 5:6, :]
    o_ref[0] = y + g2 * acc


def _post(attn, cu, gates, x, mod3, n2g, w_attn_proj, w_conv_proj, w_out, w1, w2, tm):
    B, S, D = x.shape
    tok = lambda w: pl.BlockSpec((1, tm, w), lambda b, i: (b, i, 0))
    return pl.pallas_call(
        _post_kernel,
        grid=(B, S // tm),
        in_specs=[tok(D), tok(D), tok(2 * D), tok(D),
                  pl.BlockSpec((1, N_ADA, D), lambda b, i: (b, 0, 0)),
                  _const_spec((1, D)), _const_spec((D, D)), _const_spec((D, D)), _const_spec((D, D)),
                  _const_spec(w1.shape), _const_spec(w2.shape)],
        out_specs=tok(D),
        out_shape=jax.ShapeDtypeStruct((B, S, D), F32),
        compiler_params=_cparams(2),
        name="post",
    )(attn, cu, gates, x, mod3, n2g.reshape(1, D),
      w_attn_proj.astype(BF16), w_conv_proj.astype(BF16), w_out.astype(BF16), w1.astype(BF16), w2.astype(BF16))


def kernel(x, c, w_ada, b_ada, norm1_g, w_in, b_forget, q_norm_g, k_norm_g, w_attn_proj, conv_w, conv_b,
           conv_ln_g, conv_ln_b, w_conv_proj, w_out, norm2_g, w_mlp1, w_mlp2):
    B, S, D = x.shape
    assert D == N_HEADS * HEAD_DIM and w_ada.shape[0] == 1
    tm_in = min(256, S)
    t_attn = min(1024, S)
    t_tok = min(512, S)
    mod3 = _ada(c, w_ada[0], b_ada[0]).reshape(B, N_ADA, D)
    q, k, v, cu, gates = _inproj(x, mod3, norm1_g[0], w_in[0], b_forget[0], q_norm_g[0], k_norm_g[0],
                                 conv_w[0], conv_b[0], conv_ln_g[0], conv_ln_b[0], tm_in)
    attn = _attention(q, k, v, t_attn)
    return _post(attn, cu, gates, x, mod3, norm2_g[0], w_attn_proj[0], w_conv_proj[0], w_out[0],
                 w_mlp1[0], w_mlp2[0], t_tok)
```

```python
import functools

import jax
import jax.numpy as jnp
from jax import lax
from jax.experimental import pallas as pl
from jax.experimental.pallas import tpu as pltpu

F32 = jnp.float32
BF16 = jnp.bfloat16

N_HEADS = 16
HEAD_DIM = 64
CONV_K = 31
N_ADA = 6
NORM_EPS = 1e-6
LOG2E = 1.4426950408889634
NEG_BIG = -1e30

LANES = 128
SUBLANES = 8
CONV_ROWS = 64
CONV_GROUP = 4
MXU_DIM = 256
HALO = 32
VMEM_LIMIT = 56 * 1024 * 1024


def _cparams(n_axes, vmem=VMEM_LIMIT):
    return pltpu.CompilerParams(dimension_semantics=("arbitrary",) * n_axes, vmem_limit_bytes=vmem)


def _const_spec(shape):
    nd = len(shape)
    return pl.BlockSpec(shape, lambda *_: (0,) * nd, pipeline_mode=pl.Buffered(1))


def _sigmoid(x):
    return 1.0 / (1.0 + jnp.exp(-x))


def _dot(a, b):
    return jnp.dot(a, b, preferred_element_type=F32)


def _split3(x):
    hi = x.astype(BF16).astype(F32)
    r = x - hi
    mid = r.astype(BF16).astype(F32)
    lo = r - mid
    return hi, mid, lo


def _ada_kernel(c_ref, w_ref, b_ref, o_ref):
    c = c_ref[...]
    o_ref[...] = _dot(c * _sigmoid(c), w_ref[...]) + b_ref[...]


def _ada(c, w_ada, b_ada):
    B, D = c.shape
    N = w_ada.shape[1]
    tn = 1024
    return pl.pallas_call(
        _ada_kernel,
        grid=(N // tn,),
        in_specs=[pl.BlockSpec((B, D), lambda j: (0, 0)),
                  pl.BlockSpec((D, tn), lambda j: (0, j)),
                  pl.BlockSpec((1, tn), lambda j: (0, j))],
        out_specs=pl.BlockSpec((B, tn), lambda j: (0, j)),
        out_shape=jax.ShapeDtypeStruct((B, N), F32),
        compiler_params=_cparams(1),
        name="ada",
    )(c, w_ada, b_ada.reshape(1, N))


def _pack_pair(lo, hi):
    lo_bits = lax.shift_right_logical(pltpu.bitcast(lo.astype(BF16).astype(F32), jnp.uint32), jnp.uint32(16))
    hi_bits = pltpu.bitcast(hi.astype(BF16).astype(F32), jnp.uint32) & jnp.uint32(0xFFFF0000)
    return lo_bits | hi_bits


def _unpack_pair(words):
    lo = pltpu.bitcast(lax.shift_left(words, jnp.uint32(16)), F32)
    hi = pltpu.bitcast(words & jnp.uint32(0xFFFF0000), F32)
    return lo, hi


def _conv_module(u, halo_ref, sh_ref, y_ref, cw_ref, cb_ref):
    tm, C = u.shape
    n = tm + HALO
    ext = jnp.concatenate([halo_ref[...], u], axis=0)
    halo_ref[...] = u[tm - HALO:, :]
    pairs = C // (2 * LANES)
    extp = jnp.concatenate([_pack_pair(ext[:, (2 * p) * LANES:(2 * p + 1) * LANES],
                                       ext[:, (2 * p + 1) * LANES:(2 * p + 2) * LANES]) for p in range(pairs)], axis=1)
    sh_ref[0] = extp
    for r in range(1, SUBLANES):
        sh_ref[r] = pltpu.roll(extp, n - r, 0)
    for p in range(pairs):
        lanes = slice(p * LANES, (p + 1) * LANES)
        lo_l = slice((2 * p) * LANES, (2 * p + 1) * LANES)
        hi_l = slice((2 * p + 1) * LANES, (2 * p + 2) * LANES)
        wp = _pack_pair(cw_ref[:, lo_l], cw_ref[:, hi_l])
        wv = [pltpu.bitcast(jnp.broadcast_to(wp[kk:kk + 1, :], (CONV_ROWS, LANES)), BF16) for kk in range(CONV_K)]
        for r0 in range(0, tm, CONV_ROWS):
            acc_lo = jnp.broadcast_to(cb_ref[:, lo_l], (CONV_ROWS, LANES))
            acc_hi = jnp.broadcast_to(cb_ref[:, hi_l], (CONV_ROWS, LANES))
            for k0 in range(0, CONV_K, CONV_GROUP):
                part = None
                for kk in range(k0, min(k0 + CONV_GROUP, CONV_K)):
                    a, r = divmod(HALO - (CONV_K - 1) + kk, SUBLANES)
                    xs = pltpu.bitcast(sh_ref[r, r0 + SUBLANES * a:r0 + SUBLANES * a + CONV_ROWS, lanes], BF16)
                    term = wv[kk] * xs
                    part = term if part is None else part + term
                lo, hi = _unpack_pair(pltpu.bitcast(part, jnp.uint32))
                acc_lo = acc_lo + lo
                acc_hi = acc_hi + hi
            y_ref[r0:r0 + CONV_ROWS, lo_l] = acc_lo
            y_ref[r0:r0 + CONV_ROWS, hi_l] = acc_hi


def _inproj_kernel(x_ref, mod_ref, n1g_ref, wq_ref, wk_ref, wv_ref, wf_ref, wga_ref, wgb_ref, wgate_ref,
                   bf_ref, gq_ref, gk_ref, g256_ref, tri_ref, pq_ref, pk_ref,
                   cw_ref, cb_ref, lg_ref, lb_ref,
                   q_out, k_out, v_out, cu_out, gate_out, carry_ref, halo_ref, sh_ref, y_ref):
    tm = x_ref.shape[1]

    @pl.when(pl.program_id(1) == 0)
    def _():
        carry_ref[...] = jnp.zeros_like(carry_ref)
        halo_ref[...] = jnp.zeros_like(halo_ref)

    x = x_ref[0]
    ms = jnp.mean(x * x, axis=-1, keepdims=True)
    sh1 = mod_ref[0, 0:1, :]
    sc1 = mod_ref[0, 1:2, :]
    h = (x * lax.rsqrt(ms + NORM_EPS) * n1g_ref[...]) * (1.0 + sc1) + sh1
    hb = h.astype(BF16)

    u = _dot(hb, wga_ref[...]) * _sigmoid(_dot(hb, wgb_ref[...]))
    _conv_module(u, halo_ref, sh_ref, y_ref, cw_ref, cb_ref)
    yc = y_ref[...]
    mu = jnp.mean(yc, axis=-1, keepdims=True)
    dc = yc - mu
    var = jnp.mean(dc * dc, axis=-1, keepdims=True)
    yn = dc * lax.rsqrt(var + NORM_EPS) * lg_ref[...] + lb_ref[...]
    cu_out[0] = (yn * _sigmoid(yn)).astype(BF16)

    lane = lax.broadcasted_iota(jnp.int32, (tm, LANES), 1)

    def pack3(v):
        hi, mid, lo = _split3(v)
        return hi + pltpu.roll(mid, N_HEADS, 1) + pltpu.roll(lo, 2 * N_HEADS, 1)

    z = _dot(hb, wf_ref[...]) + bf_ref[...]
    lf = jnp.minimum(z, 0.0) - jnp.log1p(jnp.exp(-jnp.abs(z)))
    lf = jnp.where(lane < N_HEADS, lf, 0.0)
    cs3 = _dot(tri_ref[...], pack3(lf).astype(BF16))
    cs = cs3 + pltpu.roll(cs3, LANES - N_HEADS, 1) + pltpu.roll(cs3, LANES - 2 * N_HEADS, 1)
    fc = jnp.where(lane < N_HEADS, cs, 0.0) + carry_ref[0:1, :]
    carry_ref[0:1, :] = fc[tm - 1:tm, :]

    feat = jnp.where(lane == 3 * N_HEADS, 1.0, pack3(fc * LOG2E)).astype(BF16)
    augq = _dot(feat, pq_ref[...])
    augk = _dot(feat, pk_ref[...])

    def qk_path(w_ref, g_ref, aug, out):
        a = _dot(hb, w_ref[...])
        for c in range(a.shape[1] // MXU_DIM):
            ac = a[:, c * MXU_DIM:(c + 1) * MXU_DIM]
            ss = _dot((ac * ac).astype(BF16), g256_ref[...])
            an = ac * lax.rsqrt(ss * (1.0 / HEAD_DIM) + NORM_EPS) * g_ref[:, c * MXU_DIM:(c + 1) * MXU_DIM]
            for pp in range(MXU_DIM // LANES):
                blk = an[:, pp * LANES:(pp + 1) * LANES]
                h0 = 2 * (c * (MXU_DIM // LANES) + pp)
                out[0, h0] = jnp.where(lane < HEAD_DIM, blk, aug[:, h0 * LANES:(h0 + 1) * LANES]).astype(BF16)
                out[0, h0 + 1] = jnp.where(lane < HEAD_DIM, pltpu.roll(blk, HEAD_DIM, 1),
                                           aug[:, (h0 + 1) * LANES:(h0 + 2) * LANES]).astype(BF16)

    qk_path(wq_ref, gq_ref, augq, q_out)
    qk_path(wk_ref, gk_ref, augk, k_out)

    va = _dot(hb, wv_ref[...])
    ones_lane = jnp.where(lane == HEAD_DIM, 1.0, 0.0)
    for p in range(N_HEADS // 2):
        blk = va[:, p * LANES:(p + 1) * LANES]
        v_out[0, 2 * p] = jnp.where(lane < HEAD_DIM, blk, ones_lane).astype(BF16)
        v_out[0, 2 * p + 1] = jnp.where(lane < HEAD_DIM, pltpu.roll(blk, HEAD_DIM, 1), ones_lane).astype(BF16)

    gate_out[0] = _sigmoid(_dot(hb, wgate_ref[...])).astype(BF16)


def _placement(sign_q):
    rows = jnp.arange(LANES)[:, None]
    cols = jnp.arange(N_HEADS * LANES)[None, :]
    head = cols // LANES
    slot = cols % LANES - HEAD_DIM
    part = jnp.where(sign_q, slot, slot - 3)
    is_val = (part >= 0) & (part < 3) & (rows == part * N_HEADS + head)
    one_slot = jnp.where(sign_q, slot - 3, slot)
    is_one = (one_slot >= 0) & (one_slot < 3) & (rows == 3 * N_HEADS)
    val = jnp.where(sign_q, 1.0, -1.0)
    return (jnp.where(is_val, val, 0.0) + jnp.where(is_one, 1.0, 0.0)).astype(BF16)


def _inproj(x, mod3, n1g, w_in, b_forget, q_norm_g, k_norm_g, conv_w, conv_b, ln_g, ln_b, tm):
    B, S, D = x.shape
    aw = N_HEADS * HEAD_DIM
    wq = w_in[:, :aw].astype(BF16)
    wk = w_in[:, aw:2 * aw].astype(BF16)
    wv = w_in[:, 2 * aw:3 * aw].astype(BF16)
    o = 3 * aw
    wf = jnp.pad(w_in[:, o:o + N_HEADS], ((0, 0), (0, LANES - N_HEADS))).astype(BF16)
    o += N_HEADS
    wga = w_in[:, o:o + D].astype(BF16)
    wgb = w_in[:, o + D:o + 2 * D].astype(BF16)
    o += 2 * D
    wgate = w_in[:, o:o + 2 * D].astype(BF16)
    bf = jnp.pad(b_forget, (0, LANES - N_HEADS)).reshape(1, LANES)
    gq = (jnp.tile(q_norm_g, N_HEADS) * (HEAD_DIM ** -0.5 * LOG2E)).reshape(1, aw)
    gk = jnp.tile(k_norm_g, N_HEADS).reshape(1, aw)
    r = jnp.arange(MXU_DIM)
    g256 = (r[:, None] // HEAD_DIM == r[None, :] // HEAD_DIM).astype(BF16)
    t = jnp.arange(tm)
    tri = (t[None, :] <= t[:, None]).astype(BF16)
    pq = _placement(True)
    pk = _placement(False)
    cw = jnp.pad(conv_w, ((0, HALO - CONV_K), (0, 0)))

    head_spec = pl.BlockSpec((1, N_HEADS, tm, LANES), lambda b, i: (b, 0, i, 0))
    head_shape = jax.ShapeDtypeStruct((B, N_HEADS, S, LANES), BF16)
    return pl.pallas_call(
        _inproj_kernel,
        grid=(B, S // tm),
        in_specs=[pl.BlockSpec((1, tm, D), lambda b, i: (b, i, 0)),
                  pl.BlockSpec((1, N_ADA, D), lambda b, i: (b, 0, 0)),
                  _const_spec((1, D)),
                  _const_spec(wq.shape), _const_spec(wk.shape), _const_spec(wv.shape), _const_spec(wf.shape),
                  _const_spec(wga.shape), _const_spec(wgb.shape), _const_spec(wgate.shape),
                  _const_spec(bf.shape), _const_spec(gq.shape), _const_spec(gk.shape),
                  _const_spec(g256.shape), _const_spec(tri.shape), _const_spec(pq.shape), _const_spec(pk.shape),
                  _const_spec(cw.shape), _const_spec((1, D)), _const_spec((1, D)), _const_spec((1, D))],
        out_specs=[head_spec, head_spec, head_spec,
                   pl.BlockSpec((1, tm, D), lambda b, i: (b, i, 0)),
                   pl.BlockSpec((1, tm, 2 * D), lambda b, i: (b, i, 0))],
        out_shape=[head_shape, head_shape, head_shape,
                   jax.ShapeDtypeStruct((B, S, D), BF16),
                   jax.ShapeDtypeStruct((B, S, 2 * D), BF16)],
        scratch_shapes=[pltpu.VMEM((SUBLANES, LANES), F32), pltpu.VMEM((HALO, D), F32),
                        pltpu.VMEM((SUBLANES, tm + HALO, D // 2), jnp.uint32), pltpu.VMEM((tm, D), F32)],
        compiler_params=_cparams(2),
        name="inproj",
    )(x, mod3, n1g.reshape(1, D), wq, wk, wv, wf, wga, wgb, wgate, bf, gq, gk, g256, tri, pq, pk,
      cw, conv_b.reshape(1, D), ln_g.reshape(1, D), ln_b.reshape(1, D))


def _attn_kernel(q_ref, k_ref, v_ref, o_ref, *, tq):
    S = q_ref.shape[2]
    half = tq // 2
    lane = lax.broadcasted_iota(jnp.int32, (tq, LANES), 1)
    row_t = lax.broadcasted_iota(jnp.int32, (half, half), 0)
    col_t = lax.broadcasted_iota(jnp.int32, (half, half), 1)
    row_b = lax.broadcasted_iota(jnp.int32, (half, tq), 0) + half
    col_b = lax.broadcasted_iota(jnp.int32, (half, tq), 1)

    def scores(q, k):
        return lax.dot_general(q, k, (((1,), (1,)), ((), ())), preferred_element_type=F32)

    def online_softmax(s, m, acc, v):
        m_new = jnp.maximum(m, jnp.max(s, axis=-1, keepdims=True))
        p = jnp.exp2(s - m_new)
        return m_new, jnp.exp2(m - m_new) * acc + _dot(p.astype(BF16), v)

    for i in range(S // tq):
        rows = slice(i * tq, (i + 1) * tq)
        qs = [q_ref[0, e, rows, :] for e in range(2)]
        carry = [(jnp.full((tq, 1), NEG_BIG, F32), jnp.zeros((tq, LANES), F32))] * 2
        for j in range(i):
            keys = slice(j * tq, (j + 1) * tq)
            carry = [online_softmax(scores(qs[e], k_ref[0, e, keys, :]), *carry[e], v_ref[0, e, keys, :])
                     for e in range(2)]
        outs = []
        for e in range(2):
            m, acc = carry[e]
            k = k_ref[0, e, rows, :]
            v = v_ref[0, e, rows, :]
            s_t = jnp.where(col_t <= row_t, scores(qs[e][:half], k[:half]), NEG_BIG)
            _, acc_t = online_softmax(s_t, m[:half], acc[:half], v[:half])
            s_b = jnp.where(col_b <= row_b, scores(qs[e][half:], k), NEG_BIG)
            _, acc_b = online_softmax(s_b, m[half:], acc[half:], v)
            acc = jnp.concatenate([acc_t, acc_b], axis=0)
            outs.append(acc / acc[:, HEAD_DIM:HEAD_DIM + 1])
        o_ref[0, rows, :] = jnp.where(lane < HEAD_DIM, outs[0], pltpu.roll(outs[1], HEAD_DIM, 1)).astype(BF16)


def _attention(q, k, v, tq):
    B, H, S, _ = q.shape
    pair_spec = pl.BlockSpec((1, 2, S, LANES), lambda b, p: (b, p, 0, 0))
    return pl.pallas_call(
        functools.partial(_attn_kernel, tq=tq),
        grid=(B, H // 2),
        in_specs=[pair_spec, pair_spec, pair_spec],
        out_specs=pl.BlockSpec((1, S, LANES), lambda b, p: (b, 0, p)),
        out_shape=jax.ShapeDtypeStruct((B, S, H * HEAD_DIM), BF16),
        compiler_params=_cparams(2),
        name="attn",
    )(q, k, v)


def _post_kernel(a_ref, cu_ref, gate_ref, x_ref, mod_ref, n2g_ref, wa_ref, wc_ref, wo_ref, w1_ref, w2_ref, o_ref):
    D = x_ref.shape[2]
    ba = _dot(a_ref[0], wa_ref[...])
    bb = _dot(cu_ref[0], wc_ref[...])
    merged = gate_ref[0, :, :D].astype(F32) * ba + gate_ref[0, :, D:].astype(F32) * bb
    g1 = mod_ref[0, 2:3, :]
    y = x_ref[0] + g1 * _dot(merged.astype(BF16), wo_ref[...])
    ms = jnp.mean(y * y, axis=-1, keepdims=True)
    sh2 = mod_ref[0, 3:4, :]
    sc2 = mod_ref[0, 4:5, :]
    h2 = ((y * lax.rsqrt(ms + NORM_EPS) * n2g_ref[...]) * (1.0 + sc2) + sh2).astype(BF16)
    acc = jnp.zeros(y.shape, F32)
    for c in range(w1_ref.shape[1] // D):
        hc = jnp.maximum(_dot(h2, w1_ref[:, c * D:(c + 1) * D]), 0.0)
        acc = acc + _dot((hc * hc).astype(BF16), w2_ref[c * D:(c + 1) * D, :])
    g2 = mod_ref[0, 5:6, :]
    o_ref[0] = y + g2 * acc


def _post(attn, cu, gates, x, mod3, n2g, w_attn_proj, w_conv_proj, w_out, w1, w2, tm):
    B, S, D = x.shape
    tok = lambda w: pl.BlockSpec((1, tm, w), lambda b, i: (b, i, 0))
    return pl.pallas_call(
        _post_kernel,
        grid=(B, S // tm),
        in_specs=[tok(D), tok(D), tok(2 * D), tok(D),
                  pl.BlockSpec((1, N_ADA, D), lambda b, i: (b, 0, 0)),
                  _const_spec((1, D)), _const_spec((D, D)), _const_spec((D, D)), _const_spec((D, D)),
                  _const_spec(w1.shape), _const_spec(w2.shape)],
        out_specs=tok(D),
        out_shape=jax.ShapeDtypeStruct((B, S, D), F32),
        compiler_params=_cparams(2),
        name="post",
    )(attn, cu, gates, x, mod3, n2g.reshape(1, D),
      w_attn_proj.astype(BF16), w_conv_proj.astype(BF16), w_out.astype(BF16), w1.astype(BF16), w2.astype(BF16))


def kernel(x, c, w_ada, b_ada, norm1_g, w_in, b_forget, q_norm_g, k_norm_g, w_attn_proj, conv_w, conv_b,
           conv_ln_g, conv_ln_b, w_conv_proj, w_out, norm2_g, w_mlp1, w_mlp2):
    B, S, D = x.shape
    assert D == N_HEADS * HEAD_DIM and w_ada.shape[0] == 1
    tm_in = min(512, S)
    t_attn = min(1024, S)
    t_tok = min(512, S)
    mod3 = _ada(c, w_ada[0], b_ada[0]).reshape(B, N_ADA, D)
    q, k, v, cu, gates = _inproj(x, mod3, norm1_g[0], w_in[0], b_forget[0], q_norm_g[0], k_norm_g[0],
                                 conv_w[0], conv_b[0], conv_ln_g[0], conv_ln_b[0], tm_in)
    attn = _attention(q, k, v, t_attn)
    return _post(attn, cu, gates, x, mod3, norm2_g[0], w_attn_proj[0], w_conv_proj[0], w_out[0],
                 w_mlp1[0], w_mlp2[0], t_tok)
```
